```python
import jax, jax.numpy as jnp
from jax import lax
import numpy as np

D_MODEL = 1024
BATCH = 16
SEQ = 4096
DEPTH = 1
DEC_BATCH = 32
DEC_SEQ = 32
PAST_LEN = 4096

CHUNK = 64
D_MIX = D_MODEL
D_RET = D_MIX // 2
H_RET = 4
DK_RET = D_RET // H_RET
DV_RET = DK_RET
D_CONV = D_MIX - D_RET
CONV_W = 3
D_IN = 4 * D_RET + 3 * D_CONV
ROPE_THETA = 10000.0
RMS_EPS = 1e-6
PEER_HEADS = 8
N_KEYS = 128
N_EXPERTS = N_KEYS * N_KEYS
PEER_TOPK = 16
D_KEY = 256
D_KEY_HALF = D_KEY // 2
PEER_BLOCK = 256

kernel_name = "hybrid_retention_shortconv_peer_stream_step"


def rms_scale(x):
    xf = x.astype(jnp.float32)
    return xf * lax.rsqrt(jnp.mean(xf * xf, axis=-1, keepdims=True) + RMS_EPS)


def rmsnorm(x, g):
    return (rms_scale(x) * g.astype(jnp.float32)).astype(x.dtype)


def retention_log_decay():
    return np.log1p(-np.exp2(-5.0 - np.arange(H_RET))).astype(np.float32)


def rotary(x, pos):
    half = x.shape[-1] // 2
    inv = ROPE_THETA ** (-jnp.arange(half, dtype=jnp.float32) / half)
    ang = pos.astype(jnp.float32)[:, None] * inv[None, :]
    cos = jnp.cos(ang).astype(x.dtype)
    sin = jnp.sin(ang).astype(x.dtype)
    x1, x2 = x[..., :half], x[..., half:]
    return jnp.concatenate([x1 * cos - x2 * sin, x1 * sin + x2 * cos], axis=-1)


def retention_block(state, q, k, v):
    L = q.shape[2]
    lg = jnp.asarray(retention_log_decay())
    idx = jnp.arange(L, dtype=jnp.float32)
    dist = jnp.abs(idx[:, None] - idx[None, :])
    intra_decay = jnp.exp(lg[:, None, None] * dist).astype(q.dtype)
    q_decay = jnp.exp(lg[:, None] * (idx + 1.0)).astype(q.dtype)
    k_decay = jnp.exp(lg[:, None] * (L - 1.0 - idx)).astype(q.dtype)
    chunk_decay = jnp.exp(lg * L).astype(q.dtype)
    scores = jnp.einsum('bhid,bhjd->bhij', q, k) * intra_decay
    o = (jnp.einsum('bhij,bhje->bhie', scores, v)
         + jnp.einsum('bhid,bhde->bhie', q * q_decay[..., None], state))
    new_state = (chunk_decay[:, None, None] * state
                 + jnp.einsum('bhjd,bhje->bhde', k * k_decay[..., None], v))
    return o, new_state


def retention(state, q, k, v):
    B, H, L, _ = q.shape
    if L <= CHUNK:
        return retention_block(state, q, k, v)
    n = L // CHUNK

    def to_chunks(t):
        return jnp.moveaxis(t.reshape(B, H, n, CHUNK, t.shape[-1]), 2, 0)

    def step(s, qkv):
        o, s_new = retention_block(s, qkv[0], qkv[1], qkv[2])
        return s_new, o

    s_fin, o = lax.scan(step, state, (to_chunks(q), to_chunks(k), to_chunks(v)))
    o = jnp.moveaxis(o, 0, 2).reshape(B, H, L, v.shape[-1])
    return o, s_fin


def short_conv(buf, u, w):
    L = u.shape[1]
    ext = jnp.concatenate([buf.astype(u.dtype), u], axis=1)
    y = w[0] * ext[:, 0:L]
    for j in range(1, CONV_W):
        y = y + w[j] * ext[:, j:j + L]
    return y, ext[:, L:]


def peer(x, wq, keys1, keys2, u_tab, v_tab):
    shp = x.shape
    xt = x.reshape(-1, shp[-1])
    T = xt.shape[0]
    n_blk = -(-T // PEER_BLOCK)
    pad = n_blk * PEER_BLOCK - T
    xt = jnp.pad(xt, ((0, pad), (0, 0))).reshape(n_blk, PEER_BLOCK, shp[-1])

    def block(xb):
        q = (xb @ wq).reshape(PEER_BLOCK, PEER_HEADS, 2, D_KEY_HALF)
        s1 = jnp.einsum('thd,hnd->thn', q[:, :, 0], keys1)
        s2 = jnp.einsum('thd,hnd->thn', q[:, :, 1], keys2)
        v1, i1 = lax.top_k(s1, PEER_TOPK)
        v2, i2 = lax.top_k(s2, PEER_TOPK)
        cand = (v1[..., :, None] + v2[..., None, :]).reshape(PEER_BLOCK, PEER_HEADS, PEER_TOPK * PEER_TOPK)
        cidx = (i1[..., :, None] * N_KEYS + i2[..., None, :]).reshape(PEER_BLOCK, PEER_HEADS, PEER_TOPK * PEER_TOPK)
        top_s, sel = lax.top_k(cand, PEER_TOPK)
        eidx = jnp.take_along_axis(cidx, sel, axis=-1)
        gate = jax.nn.softmax(top_s.astype(jnp.float32), axis=-1).astype(xb.dtype)
        act = jax.nn.gelu(jnp.einsum('td,thkd->thk', xb, u_tab[eidx]))
        return jnp.einsum('thk,thkd->td', gate * act, v_tab[eidx])

    out = lax.map(block, xt)
    return out.reshape(-1, shp[-1])[:T].reshape(shp)


def layer(x, ret_state, conv_buf, pos0, g_mix, w_in, conv_w, w_out, g_ffn, wq, keys1, keys2, u_tab, v_tab):
    B, L, _ = x.shape
    h = rmsnorm(x, g_mix)
    proj = h @ w_in
    q, k, v, g, gb, gc, hc = jnp.split(
        proj, [D_RET, 2 * D_RET, 3 * D_RET, 4 * D_RET, 4 * D_RET + D_CONV, 4 * D_RET + 2 * D_CONV], axis=-1)

    def heads(t):
        return t.reshape(B, L, H_RET, -1).transpose(0, 2, 1, 3)

    pos = pos0 + jnp.arange(L, dtype=jnp.int32)
    q = rotary(heads(q), pos) * (DK_RET ** -0.5)
    k = rotary(heads(k), pos)
    o, ret_new = retention(ret_state, q, k, heads(v))
    o = rms_scale(o).astype(x.dtype)
    o = o.transpose(0, 2, 1, 3).reshape(B, L, D_RET) * jax.nn.silu(g)

    conv_y, conv_new = short_conv(conv_buf, gc * hc, conv_w)
    c = gb * conv_y

    x = x + jnp.concatenate([o, c], axis=-1) @ w_out
    x = x + peer(rmsnorm(x, g_ffn), wq, keys1, keys2, u_tab, v_tab)
    return x, ret_new, conv_new


def setup_inputs(seed: int = 0) -> dict:
    key = jax.random.key(seed)
    ks = jax.random.split(key, 16)
    nrm = jax.random.normal
    f32 = jnp.float32
    return {
        "x_prompt": nrm(ks[0], (BATCH, SEQ, D_MODEL), f32),
        "x_sample": nrm(ks[1], (DEC_BATCH, DEC_SEQ, D_MODEL), f32),
        "state_ret": nrm(ks[2], (DEPTH, DEC_BATCH, H_RET, DK_RET, DV_RET), f32),
        "state_conv": nrm(ks[3], (DEPTH, DEC_BATCH, CONV_W - 1, D_CONV), f32),
        "norm_mix_g": 1.0 + 0.02 * nrm(ks[4], (DEPTH, D_MODEL), f32),
        "w_in": nrm(ks[5], (DEPTH, D_MODEL, D_IN), f32) * D_MODEL ** -0.5,
        "conv_w": nrm(ks[6], (DEPTH, CONV_W, D_CONV), f32) * CONV_W ** -0.5,
        "w_out": nrm(ks[7], (DEPTH, D_MIX, D_MODEL), f32) * D_MIX ** -0.5,
        "norm_ffn_g": 1.0 + 0.02 * nrm(ks[8], (DEPTH, D_MODEL), f32),
        "peer_wq": nrm(ks[9], (DEPTH, D_MODEL, PEER_HEADS * D_KEY), f32) * D_MODEL ** -0.5,
        "peer_keys1": nrm(ks[10], (DEPTH, PEER_HEADS, N_KEYS, D_KEY_HALF), f32) * D_KEY_HALF ** -0.5,
        "peer_keys2": nrm(ks[11], (DEPTH, PEER_HEADS, N_KEYS, D_KEY_HALF), f32) * D_KEY_HALF ** -0.5,
        "peer_u": nrm(ks[12], (DEPTH, N_EXPERTS, D_MODEL), f32) * D_MODEL ** -0.5,
        "peer_v": nrm(ks[13], (DEPTH, N_EXPERTS, D_MODEL), f32) * PEER_HEADS ** -0.5,
        "norm_final_g": 1.0 + 0.02 * nrm(ks[14], (D_MODEL,), f32),
    }


def reference(x_prompt, x_sample, state_ret, state_conv, norm_mix_g, w_in, conv_w, w_out,
              norm_ffn_g, peer_wq, peer_keys1, peer_keys2, peer_u, peer_v, norm_final_g):
    yp, ys = x_prompt, x_sample
    bp = x_prompt.shape[0]
    rp_list, cp_list, rs_list, cs_list = [], [], [], []
    for l in range(DEPTH):
        w = (norm_mix_g[l], w_in[l], conv_w[l], w_out[l], norm_ffn_g[l],
             peer_wq[l], peer_keys1[l], peer_keys2[l], peer_u[l], peer_v[l])
        r0 = jnp.zeros((bp, H_RET, DK_RET, DV_RET), x_prompt.dtype)
        c0 = jnp.zeros((bp, CONV_W - 1, D_CONV), x_prompt.dtype)
        yp, rp, cp = layer(yp, r0, c0, 0, *w)
        ys, rs, cs = layer(ys, state_ret[l].astype(x_sample.dtype), state_conv[l], PAST_LEN, *w)
        rp_list.append(rp)
        cp_list.append(cp)
        rs_list.append(rs)
        cs_list.append(cs)
    y_prompt = rmsnorm(yp, norm_final_g)
    y_sample = rmsnorm(ys, norm_final_g)
    new_state_ret_prompt = jnp.stack(rp_list)
    new_state_conv_prompt = jnp.stack(cp_list)
    new_state_ret_sample = jnp.stack(rs_list)
    new_state_conv_sample = jnp.stack(cs_list)
    return (y_prompt, y_sample, new_state_ret_prompt, new_state_conv_prompt, new_state_ret_sample, new_state_conv_sample)
```

```python
import functools

import numpy as np
import jax
import jax.numpy as jnp
from jax import lax
from jax.experimental import pallas as pl
from jax.experimental.pallas import tpu as pltpu

F32 = jnp.float32
BF16 = jnp.bfloat16

H_RET = 4
DK_RET = 128
D_RET = H_RET * DK_RET
D_CONV = 512
CONV_W = 3
CHUNK = 64
ROPE_THETA = 10000.0
RMS_EPS = 1e-6
PEER_HEADS = 8
N_KEYS = 128
PEER_TOPK = 16
D_KEY_HALF = 128

ROW_BLOCK = 128
LANES = 128
SUBLANES = 8
BF16_ROWS = 16
VMEM_LIMIT = 56 * 1024 * 1024


def _log_decay():
    return np.log1p(-np.exp2(-5.0 - np.arange(H_RET))).astype(np.float32).astype(np.float64)


def _retention_tables(chunk):
    lg = _log_decay()[:, None, None]
    idx = np.arange(ROW_BLOCK)
    ci, cj = idx[:, None] // chunk, idx[None, :] // chunk
    dist = (idx[:, None] - idx[None, :]).astype(np.float64)
    same = np.exp(lg * np.abs(dist)[None])
    later = np.exp(lg * dist[None])
    mask = np.where((ci == cj)[None], same, np.where((ci > cj)[None], later, 0.0))
    return mask.astype(np.float32)


def _prompt_tables():
    lg = _log_decay()
    idx = np.arange(ROW_BLOCK, dtype=np.float64)
    mask = _retention_tables(CHUNK)
    qd = np.exp(lg[:, None] * (idx[None] + 1.0))
    kd = np.exp(lg[:, None] * (ROW_BLOCK - 1.0 - idx[None]))
    cd = np.exp(lg * ROW_BLOCK)
    bc = lambda t: np.broadcast_to(t[:, :, None], (H_RET, ROW_BLOCK, DK_RET)).astype(np.float32)
    return mask, bc(qd), bc(kd), [float(c) for c in cd]


def _sample_tables(seq):
    lg = _log_decay()
    idx = np.arange(ROW_BLOCK)
    pos = (idx % seq).astype(np.float64)
    same_seq = (idx[:, None] // seq) == (idx[None, :] // seq)
    dist = np.abs(pos[:, None] - pos[None, :])
    mask = np.where(same_seq[None], np.exp(lg[:, None, None] * dist[None]), 0.0).astype(np.float32)
    qd = np.exp(lg[:, None] * (pos[None] + 1.0))
    kd = np.exp(lg[:, None] * (seq - 1.0 - pos[None]))
    cd = np.exp(lg * seq)
    bc = lambda t: np.broadcast_to(t[:, :, None], (H_RET, ROW_BLOCK, DK_RET)).astype(np.float32)
    return mask, bc(qd), bc(kd), [float(c) for c in cd]


def _rope_tables(pos):
    half = DK_RET // 2
    inv = ROPE_THETA ** (-jnp.arange(half, dtype=F32) / half)
    ang = pos.astype(F32)[:, None] * inv[None, :]
    cos, sin = jnp.cos(ang), jnp.sin(ang)
    return jnp.concatenate([cos, cos], axis=1), jnp.concatenate([-sin, sin], axis=1)


def _rms_rows(x):
    return x * lax.rsqrt(jnp.mean(x * x, axis=-1, keepdims=True) + RMS_EPS)


def _project_and_rotate(x, gmix_ref, w_in_ref, cos_ref, sin_ref, proj_ref):
    h = (_rms_rows(x) * gmix_ref[...]).astype(BF16)
    proj_ref[...] = jnp.dot(h, w_in_ref[...], preferred_element_type=F32)
    cs, sn = cos_ref[...], sin_ref[...]
    scale = DK_RET ** -0.5
    for hh in range(2 * H_RET):
        sl = slice(hh * DK_RET, (hh + 1) * DK_RET)
        t = proj_ref[:, sl]
        t = t * cs + pltpu.roll(t, DK_RET // 2, axis=1) * sn
        proj_ref[:, sl] = t * scale if hh < H_RET else t


def _head_cols(group, h):
    start = group * D_RET + h * DK_RET
    return slice(start, start + DK_RET)


def _gate_and_store(o, g, mix_ref, rows, h):
    o = _rms_rows(o)
    mix_ref[rows, _head_cols(0, h)] = (o * (g * (1.0 / (1.0 + jnp.exp(-g))))).astype(BF16)


def _short_conv(u, prev0, prev1, pos, w_ref):
    p1 = jnp.where(pos == 0, prev1, pltpu.roll(u, 1, axis=0))
    p2 = jnp.where(pos == 0, prev0, jnp.where(pos == 1, prev1, pltpu.roll(u, 2, axis=0)))
    return w_ref[0:1, :] * p2 + w_ref[1:2, :] * p1 + w_ref[2:3, :] * u


def _mixer_prompt_kernel(cd, x_ref, cos_ref, sin_ref, gmix_ref, w_in_ref, conv_w_ref, w_out_ref,
                         mask_ref, qd_ref, kd_ref,
                         out_ref, state_ref, conv_ref,
                         proj_ref, mix_ref):
    rows_total = x_ref.shape[0]

    @pl.when(pl.program_id(1) == 0)
    def _():
        state_ref[...] = jnp.zeros_like(state_ref)
        conv_ref[...] = jnp.zeros_like(conv_ref)

    x = x_ref[...]
    _project_and_rotate(x, gmix_ref, w_in_ref, cos_ref, sin_ref, proj_ref)

    for blk in range(rows_total // ROW_BLOCK):
        rows = slice(blk * ROW_BLOCK, (blk + 1) * ROW_BLOCK)
        for h in range(H_RET):
            q = proj_ref[rows, _head_cols(0, h)]
            k = proj_ref[rows, _head_cols(1, h)]
            vb = proj_ref[rows, _head_cols(2, h)].astype(BF16)
            a = lax.dot_general(q.astype(BF16), k.astype(BF16), (((1,), (1,)), ((), ())),
                                preferred_element_type=F32)
            s = state_ref[h]
            o = (jnp.dot((a * mask_ref[h]).astype(BF16), vb, preferred_element_type=F32)
                 + jnp.dot((q * qd_ref[h]).astype(BF16), s.astype(BF16), preferred_element_type=F32))
            kt = (k * kd_ref[h]).T.astype(BF16)
            state_ref[h] = cd[h] * s + jnp.dot(kt, vb, preferred_element_type=F32)
            _gate_and_store(o, proj_ref[rows, _head_cols(3, h)], mix_ref, rows, h)

    base = 4 * D_RET
    u = proj_ref[:, base + D_CONV:base + 2 * D_CONV] * proj_ref[:, base + 2 * D_CONV:base + 3 * D_CONV]
    pos = lax.broadcasted_iota(jnp.int32, u.shape, 0)
    y = _short_conv(u, conv_ref[0:1, :], conv_ref[1:2, :], pos, conv_w_ref)
    conv_ref[...] = u[rows_total - (CONV_W - 1):, :]
    mix_ref[:, D_RET:] = (proj_ref[:, base:base + D_CONV] * y).astype(BF16)

    out_ref[...] = x + jnp.dot(mix_ref[...], w_out_ref[...], preferred_element_type=F32)


def _mixer_sample_kernel(cd, seq, x_ref, cos_ref, sin_ref, gmix_ref, w_in_ref, conv_w_ref, w_out_ref,
                         mask_ref, qd_ref, kd_ref, state_in_ref, conv_in_ref,
                         out_ref, state_ref, conv_ref,
                         proj_ref, mix_ref):
    nseq = ROW_BLOCK // seq
    x = x_ref[...]
    _project_and_rotate(x, gmix_ref, w_in_ref, cos_ref, sin_ref, proj_ref)

    rows = slice(0, ROW_BLOCK)
    row_id = lax.broadcasted_iota(jnp.int32, (ROW_BLOCK, DK_RET), 0)
    for h in range(H_RET):
        q = proj_ref[rows, _head_cols(0, h)]
        k = proj_ref[rows, _head_cols(1, h)]
        vb = proj_ref[rows, _head_cols(2, h)].astype(BF16)
        a = lax.dot_general(q.astype(BF16), k.astype(BF16), (((1,), (1,)), ((), ())),
                            preferred_element_type=F32)
        qs = (q * qd_ref[h]).astype(BF16)
        ks = k * kd_ref[h]
        inter = []
        for b in range(nseq):
            s = state_in_ref[b, h]
            inter.append(jnp.dot(qs[b * seq:(b + 1) * seq, :], s.astype(BF16), preferred_element_type=F32))
            in_seq = (row_id >= b * seq) & (row_id < (b + 1) * seq)
            kt = jnp.where(in_seq, ks, 0.0).T.astype(BF16)
            state_ref[b, h] = cd[h] * s + jnp.dot(kt, vb, preferred_element_type=F32)
        o = (jnp.dot((a * mask_ref[h]).astype(BF16), vb, preferred_element_type=F32)
             + jnp.concatenate(inter, axis=0))
        _gate_and_store(o, proj_ref[rows, _head_cols(3, h)], mix_ref, rows, h)

    base = 4 * D_RET
    u = proj_ref[:, base + D_CONV:base + 2 * D_CONV] * proj_ref[:, base + 2 * D_CONV:base + 3 * D_CONV]
    pos = lax.broadcasted_iota(jnp.int32, u.shape, 0) % seq
    prev0 = jnp.concatenate([jnp.broadcast_to(conv_in_ref[b, 0:1, :], (seq, D_CONV)) for b in range(nseq)], axis=0)
    prev1 = jnp.concatenate([jnp.broadcast_to(conv_in_ref[b, 1:2, :], (seq, D_CONV)) for b in range(nseq)], axis=0)
    y = _short_conv(u, prev0, prev1, pos, conv_w_ref)
    for b in range(nseq):
        conv_ref[b] = u[(b + 1) * seq - (CONV_W - 1):(b + 1) * seq, :]
    mix_ref[:, D_RET:] = (proj_ref[:, base:base + D_CONV] * y).astype(BF16)

    out_ref[...] = x + jnp.dot(mix_ref[...], w_out_ref[...], preferred_element_type=F32)


def _full(shape):
    return pl.BlockSpec(shape, lambda *_: (0,) * len(shape))


def _mixer_prompt(x, gmix, w_in, conv_w, w_out):
    batch, seq, d = x.shape
    d_in = w_in.shape[1]
    rows = 512 if seq % 512 == 0 else ROW_BLOCK
    assert seq % rows == 0 and seq > CHUNK and ROW_BLOCK % CHUNK == 0
    mask, qd, kd, cd = _prompt_tables()
    cos, sin = _rope_tables(jnp.arange(seq, dtype=jnp.int32))
    hshape = (H_RET, ROW_BLOCK, DK_RET)
    return pl.pallas_call(
        functools.partial(_mixer_prompt_kernel, cd),
        grid=(batch, seq // rows),
        in_specs=[
            pl.BlockSpec((None, rows, d), lambda b, l: (b, l, 0)),
            pl.BlockSpec((rows, DK_RET), lambda b, l: (l, 0)),
            pl.BlockSpec((rows, DK_RET), lambda b, l: (l, 0)),
            _full((1, d)), _full((d, d_in)), _full((CONV_W, D_CONV)), _full((d, d)),
            _full(hshape), _full(hshape), _full(hshape),
        ],
        out_specs=[
            pl.BlockSpec((None, rows, d), lambda b, l: (b, l, 0)),
            pl.BlockSpec((None, H_RET, DK_RET, DK_RET), lambda b, l: (b, 0, 0, 0)),
            pl.BlockSpec((None, CONV_W - 1, D_CONV), lambda b, l: (b, 0, 0)),
        ],
        out_shape=[
            jax.ShapeDtypeStruct((batch, seq, d), F32),
            jax.ShapeDtypeStruct((batch, H_RET, DK_RET, DK_RET), F32),
            jax.ShapeDtypeStruct((batch, CONV_W - 1, D_CONV), F32),
        ],
        scratch_shapes=[pltpu.VMEM((rows, d_in), F32), pltpu.VMEM((rows, d), BF16)],
        compiler_params=pltpu.CompilerParams(
            dimension_semantics=("parallel", "arbitrary"), vmem_limit_bytes=VMEM_LIMIT),
        name="mixer_prompt",
    )(x, cos, sin, gmix, w_in, conv_w, w_out, jnp.asarray(mask), jnp.asarray(qd), jnp.asarray(kd))


def _mixer_sample(x, state, conv_state, past_len, gmix, w_in, conv_w, w_out):
    batch, seq, d = x.shape
    d_in = w_in.shape[1]
    assert seq <= CHUNK and ROW_BLOCK % seq == 0 and seq >= CONV_W - 1
    nseq = ROW_BLOCK // seq
    assert batch % nseq == 0
    mask, qd, kd, cd = _sample_tables(seq)
    cos, sin = _rope_tables(past_len + jnp.tile(jnp.arange(seq, dtype=jnp.int32), nseq))
    hshape = (H_RET, ROW_BLOCK, DK_RET)
    x2 = x.reshape(batch * seq, d)
    out, new_state, new_conv = pl.pallas_call(
        functools.partial(_mixer_sample_kernel, cd, seq),
        grid=(batch // nseq,),
        in_specs=[
            pl.BlockSpec((ROW_BLOCK, d), lambda g: (g, 0)),
            _full((ROW_BLOCK, DK_RET)), _full((ROW_BLOCK, DK_RET)),
            _full((1, d)), _full((d, d_in)), _full((CONV_W, D_CONV)), _full((d, d)),
            _full(hshape), _full(hshape), _full(hshape),
            pl.BlockSpec((nseq, H_RET, DK_RET, DK_RET), lambda g: (g, 0, 0, 0)),
            pl.BlockSpec((nseq, CONV_W - 1, D_CONV), lambda g: (g, 0, 0)),
        ],
        out_specs=[
            pl.BlockSpec((ROW_BLOCK, d), lambda g: (g, 0)),
            pl.BlockSpec((nseq, H_RET, DK_RET, DK_RET), lambda g: (g, 0, 0, 0)),
            pl.BlockSpec((nseq, CONV_W - 1, D_CONV), lambda g: (g, 0, 0)),
        ],
        out_shape=[
            jax.ShapeDtypeStruct((batch * seq, d), F32),
            jax.ShapeDtypeStruct((batch, H_RET, DK_RET, DK_RET), F32),
            jax.ShapeDtypeStruct((batch, CONV_W - 1, D_CONV), F32),
        ],
        scratch_shapes=[pltpu.VMEM((ROW_BLOCK, d_in), F32), pltpu.VMEM((ROW_BLOCK, d), BF16)],
        compiler_params=pltpu.CompilerParams(
            dimension_semantics=("parallel",), vmem_limit_bytes=VMEM_LIMIT),
        name="mixer_sample",
    )(x2, cos, sin, gmix, w_in, conv_w, w_out, jnp.asarray(mask), jnp.asarray(qd), jnp.asarray(kd),
      state, conv_state)
    return out, new_state, new_conv


def _sort16_pairs():
    def merge(lo, hi, r):
        step = r * 2
        if step < hi - lo:
            yield from merge(lo, hi, step)
            yield from merge(lo + r, hi, step)
            for i in range(lo + r, hi - r, step):
                yield (i, i + r)
        else:
            yield (lo, lo + r)

    def sort(lo, hi):
        if hi - lo >= 1:
            mid = lo + (hi - lo) // 2
            yield from sort(lo, mid)
            yield from sort(mid + 1, hi)
            yield from merge(lo, hi, 1)

    return tuple(sort(0, PEER_TOPK - 1))


_SORT16 = _sort16_pairs()


def _exchange(vals, i, j):
    a, b = vals[i], vals[j]
    if b is None:
        return
    if a is None:
        vals[i], vals[j] = b, None
        return
    vals[i], vals[j] = jnp.maximum(a, b), jnp.minimum(a, b)


def _sort16_desc(vals):
    vals = list(vals)
    for i, j in _SORT16:
        _exchange(vals, i, j)
    return vals


def _merge_top16(a, b):
    vals = []
    for k in range(PEER_TOPK):
        x, y = a[k], b[PEER_TOPK - 1 - k]
        vals.append(y if x is None else (x if y is None else jnp.maximum(x, y)))
    stride = PEER_TOPK // 2
    while stride:
        for i in range(PEER_TOPK):
            if i & stride == 0:
                _exchange(vals, i, i + stride)
        stride //= 2
    return vals


def _top16_of_rows(s):
    vals = _sort16_desc([s[r * SUBLANES:(r + 1) * SUBLANES, :] for r in range(N_KEYS // SUBLANES)])
    shift = SUBLANES // 2
    while shift:
        vals = _merge_top16(vals, [pltpu.roll(v, shift, axis=0) for v in vals])
        shift //= 2
    return vals


_REST_PAIRS = tuple((k1, k2) for k1 in range(2, PEER_TOPK) for k2 in range(1, PEER_TOPK)
                    if (k1 + 1) * (k2 + 1) <= PEER_TOPK)
assert len(_REST_PAIRS) <= PEER_TOPK


def _top16_sums(v1, v2):
    pad = lambda lst: lst + [None] * (PEER_TOPK - len(lst))
    row0 = [v1[0] + v2[k] for k in range(PEER_TOPK)]
    col0 = pad([v1[k] + v2[0] for k in range(1, PEER_TOPK)])
    row1 = pad([v1[1] + v2[k] for k in range(1, PEER_TOPK) if 2 * (k + 1) <= PEER_TOPK])
    rest = _sort16_desc(pad([v1[a] + v2[b] for a, b in _REST_PAIRS]))
    return _merge_top16(_merge_top16(row0, col0), _merge_top16(row1, rest))


def _route_lane_block(s1_ref, s2_ref, a_ref, c_ref, b_ref, r_ref, h, lanes):
    s1 = s1_ref[h, :, lanes]
    s2 = s2_ref[h, :, lanes]
    v1 = _top16_of_rows(s1)
    v2 = _top16_of_rows(s2)
    top = _top16_sums(v1, v2)
    tau = top[PEER_TOPK - 1]
    z = jnp.exp(top[0] - top[0])
    for k in range(1, PEER_TOPK):
        z = z + jnp.exp(top[k] - top[0])
    inv_z = 1.0 / z
    for r in range(N_KEYS // SUBLANES):
        rs = slice(r * SUBLANES, (r + 1) * SUBLANES)
        x1, x2 = s1[rs, :], s2[rs, :]
        cnt = jnp.zeros_like(x1)
        rank = jnp.zeros_like(x2)
        for k in range(PEER_TOPK):
            cnt = cnt + jnp.where(x1 + v2[k] >= tau, 1.0, 0.0)
            rank = rank + jnp.where(v2[k] > x2, 1.0, 0.0)
        a_ref[h, rs, lanes] = jnp.exp(x1 - v1[0]) * inv_z
        c_ref[h, rs, lanes] = cnt
        b_ref[h, rs, lanes] = jnp.exp(x2 - v2[0])
        r_ref[h, rs, lanes] = rank


def _gelu_tanh(x):
    return 0.5 * x * (1.0 + jnp.tanh(np.sqrt(2.0 / np.pi).astype(np.float32) * (x + 0.044715 * (x * x * x))))


def _peer_kernel(rows_per_step, x_ref, gffn_ref, wq_ref, k1_ref, k2_ref, u_ref, vt_ref, gfin_ref,
                 y_ref,
                 ht_ref, acc_ref, s1_ref, s2_ref, a_ref, c_ref, b_ref, r_ref, bb_ref, rb_ref,
                 act_ref, wact_ref):
    e = pl.program_id(1)
    tokens = x_ref.shape[0]
    n_lane_blocks = tokens // LANES

    @pl.when(e == 0)
    def _route():
        h2 = _rms_rows(x_ref[...]) * gffn_ref[...]
        ht_ref[...] = h2.T.astype(BF16)
        for h in range(PEER_HEADS):
            qt = jnp.dot(wq_ref[h * 2 * D_KEY_HALF:(h + 1) * 2 * D_KEY_HALF, :], ht_ref[...],
                         preferred_element_type=F32)
            s1_ref[h] = jnp.dot(k1_ref[h], qt[:D_KEY_HALF].astype(BF16), preferred_element_type=F32)
            s2_ref[h] = jnp.dot(k2_ref[h], qt[D_KEY_HALF:].astype(BF16), preferred_element_type=F32)

        def body(it, carry):
            h = it // n_lane_blocks
            lanes = pl.ds(pl.multiple_of((it % n_lane_blocks) * LANES, LANES), LANES)
            _route_lane_block(s1_ref, s2_ref, a_ref, c_ref, b_ref, r_ref, h, lanes)
            return carry

        lax.fori_loop(0, PEER_HEADS * n_lane_blocks, body, 0)
        bb_ref[...] = b_ref[...].astype(BF16).reshape(bb_ref.shape)
        rb_ref[...] = r_ref[...].astype(BF16).reshape(rb_ref.shape)

    act_ref[...] = jnp.dot(u_ref[...], ht_ref[...], preferred_element_type=F32)

    row0 = pl.multiple_of(e * rows_per_step, SUBLANES)
    groups = N_KEYS // BF16_ROWS

    def gate_body(lb, carry):
        lanes = pl.ds(pl.multiple_of(lb * LANES, LANES), LANES)
        for ii in range(rows_per_step):
            w = None
            for h in range(PEER_HEADS):
                c_row = c_ref[h, pl.ds(row0, rows_per_step), lanes][ii:ii + 1, :]
                a_row = a_ref[h, pl.ds(row0, rows_per_step), lanes][ii:ii + 1, :]
                c16 = jnp.broadcast_to(c_row, (BF16_ROWS, LANES)).astype(BF16)
                a16 = jnp.broadcast_to(a_row, (BF16_ROWS, LANES)).astype(BF16)
                sel = jnp.where(rb_ref[h, :, :, lanes] < c16[None], bb_ref[h, :, :, lanes], jnp.zeros((), BF16))
                term = sel * a16[None]
                w = term if w is None else w + term
            rs = slice(ii * N_KEYS, (ii + 1) * N_KEYS)
            act = _gelu_tanh(act_ref[rs, lanes]).astype(BF16).reshape(groups, BF16_ROWS, LANES)
            wact_ref[rs, lanes] = (w * act).reshape(N_KEYS, LANES)
        return carry

    lax.fori_loop(0, n_lane_blocks, gate_body, 0)

    contrib = jnp.dot(vt_ref[...], wact_ref[...], preferred_element_type=F32)

    @pl.when(e == 0)
    def _():
        acc_ref[...] = contrib

    @pl.when(e > 0)
    def _():
        acc_ref[...] += contrib

    @pl.when(e == pl.num_programs(1) - 1)
    def _():
        out = x_ref[...] + acc_ref[...].T
        y_ref[...] = _rms_rows(out) * gfin_ref[...]


def _peer(x, gffn, wq_t, keys1, keys2, u_tab, v_tab_t, gfin):
    total, d = x.shape
    n_experts = u_tab.shape[0]
    tokens = next(t for t in (512, 256, 128) if total % t == 0)
    rows_per_step = SUBLANES
    experts_per_step = rows_per_step * N_KEYS
    groups = N_KEYS // BF16_ROWS
    head_shape = (PEER_HEADS, N_KEYS, tokens)
    return pl.pallas_call(
        functools.partial(_peer_kernel, rows_per_step),
        grid=(total // tokens, n_experts // experts_per_step),
        in_specs=[
            pl.BlockSpec((tokens, d), lambda t, e: (t, 0)),
            _full((1, d)),
            _full(wq_t.shape), _full(keys1.shape), _full(keys2.shape),
            pl.BlockSpec((experts_per_step, d), lambda t, e: (e, 0)),
            pl.BlockSpec((d, experts_per_step), lambda t, e: (0, e)),
            _full((1, d)),
        ],
        out_specs=pl.BlockSpec((tokens, d), lambda t, e: (t, 0)),
        out_shape=jax.ShapeDtypeStruct((total, d), F32),
        scratch_shapes=[
            pltpu.VMEM((d, tokens), BF16),
            pltpu.VMEM((d, tokens), F32),
            pltpu.VMEM(head_shape, F32), pltpu.VMEM(head_shape, F32),
            pltpu.VMEM(head_shape, F32), pltpu.VMEM(head_shape, F32),
            pltpu.VMEM(head_shape, F32), pltpu.VMEM(head_shape, F32),
            pltpu.VMEM((PEER_HEADS, groups, BF16_ROWS, tokens), BF16),
            pltpu.VMEM((PEER_HEADS, groups, BF16_ROWS, tokens), BF16),
            pltpu.VMEM((experts_per_step, tokens), F32),
            pltpu.VMEM((experts_per_step, tokens), BF16),
        ],
        compiler_params=pltpu.CompilerParams(
            dimension_semantics=("parallel", "arbitrary"), vmem_limit_bytes=VMEM_LIMIT),
        name="peer",
    )(x, gffn, wq_t, keys1, keys2, u_tab, v_tab_t, gfin)


def kernel(x_prompt, x_sample, state_ret, state_conv, norm_mix_g, w_in, conv_w, w_out, norm_ffn_g,
           peer_wq, peer_keys1, peer_keys2, peer_u, peer_v, norm_final_g):
    depth = w_in.shape[0]
    assert depth == 1
    batch, seq, d = x_prompt.shape
    dec_batch, dec_seq, _ = x_sample.shape
    past_len = 4096

    gmix = norm_mix_g[0][None, :]
    gffn = norm_ffn_g[0][None, :]
    gfin = norm_final_g[None, :]
    w_in_b = w_in[0].astype(BF16)
    w_out_b = w_out[0].astype(BF16)
    wq_t = peer_wq[0].T.astype(BF16)
    keys1 = peer_keys1[0].astype(BF16)
    keys2 = peer_keys2[0].astype(BF16)
    u_tab = peer_u[0].astype(BF16)
    v_tab_t = peer_v[0].T.astype(BF16)

    xp, ret_p, conv_p = _mixer_prompt(x_prompt, gmix, w_in_b, conv_w[0], w_out_b)
    xs, ret_s, conv_s = _mixer_sample(x_sample, state_ret[0], state_conv[0], past_len,
                                      gmix, w_in_b, conv_w[0], w_out_b)

    x_all = jnp.concatenate([xp.reshape(batch * seq, d), xs], axis=0)
    y_all = _peer(x_all, gffn, wq_t, keys1, keys2, u_tab, v_tab_t, gfin)
    y_prompt = y_all[:batch * seq].reshape(batch, seq, d)
    y_sample = y_all[batch * seq:].reshape(dec_batch, dec_seq, d)
    return (y_prompt, y_sample, ret_p[None], conv_p[None], ret_s[None], conv_s[None])
```

```python
import functools

import numpy as np
import jax
import jax.numpy as jnp
from jax import lax
from jax.experimental import pallas as pl
from jax.experimental.pallas import tpu as pltpu

F32 = jnp.float32
BF16 = jnp.bfloat16

H_RET = 4
DK_RET = 128
D_RET = H_RET * DK_RET
D_CONV = 512
CONV_W = 3
CHUNK = 64
ROPE_THETA = 10000.0
RMS_EPS = 1e-6
PEER_HEADS = 8
N_KEYS = 128
PEER_TOPK = 16
D_KEY_HALF = 128
PAST_LEN = 4096

ROW_BLOCK = 128
LANES = 128
SUBLANES = 8
BF16_ROWS = 16
VMEM_LIMIT = 56 * 1024 * 1024


def _log_decay():
    return np.log1p(-np.exp2(-5.0 - np.arange(H_RET))).astype(np.float32).astype(np.float64)


def _retention_tables(chunk):
    lg = _log_decay()[:, None, None]
    idx = np.arange(ROW_BLOCK)
    ci, cj = idx[:, None] // chunk, idx[None, :] // chunk
    dist = (idx[:, None] - idx[None, :]).astype(np.float64)
    same = np.exp(lg * np.abs(dist)[None])
    later = np.exp(lg * dist[None])
    mask = np.where((ci == cj)[None], same, np.where((ci > cj)[None], later, 0.0))
    return mask.astype(np.float32)


def _prompt_tables():
    lg = _log_decay()
    idx = np.arange(ROW_BLOCK, dtype=np.float64)
    mask = _retention_tables(CHUNK)
    qd = np.exp(lg[:, None] * (idx[None] + 1.0))
    kd = np.exp(lg[:, None] * (ROW_BLOCK - 1.0 - idx[None]))
    cd = np.exp(lg * ROW_BLOCK)
    bc = lambda t: np.broadcast_to(t[:, :, None], (H_RET, ROW_BLOCK, DK_RET)).astype(np.float32)
    return mask, bc(qd), bc(kd), [float(c) for c in cd]


def _sample_tables(seq):
    lg = _log_decay()
    idx = np.arange(ROW_BLOCK)
    pos = (idx % seq).astype(np.float64)
    same_seq = (idx[:, None] // seq) == (idx[None, :] // seq)
    dist = np.abs(pos[:, None] - pos[None, :])
    mask = np.where(same_seq[None], np.exp(lg[:, None, None] * dist[None]), 0.0).astype(np.float32)
    qd = np.exp(lg[:, None] * (pos[None] + 1.0))
    kd = np.exp(lg[:, None] * (seq - 1.0 - pos[None]))
    cd = np.exp(lg * seq)
    bc = lambda t: np.broadcast_to(t[:, :, None], (H_RET, ROW_BLOCK, DK_RET)).astype(np.float32)
    return mask, bc(qd), bc(kd), [float(c) for c in cd]


def _rope_tables(pos):
    half = DK_RET // 2
    inv = ROPE_THETA ** (-jnp.arange(half, dtype=F32) / half)
    ang = pos.astype(F32)[:, None] * inv[None, :]
    cos, sin = jnp.cos(ang), jnp.sin(ang)
    return jnp.concatenate([cos, cos], axis=1), jnp.concatenate([-sin, sin], axis=1)


def _rms_rows(x):
    return x * lax.rsqrt(jnp.mean(x * x, axis=-1, keepdims=True) + RMS_EPS)


def _project_and_rotate(x, gmix_ref, w_in_ref, cos_ref, sin_ref, proj_ref):
    h = (_rms_rows(x) * gmix_ref[...]).astype(BF16)
    proj_ref[...] = jnp.dot(h, w_in_ref[...], preferred_element_type=F32)
    cs, sn = cos_ref[...], sin_ref[...]
    scale = DK_RET ** -0.5
    for hh in range(2 * H_RET):
        sl = slice(hh * DK_RET, (hh + 1) * DK_RET)
        t = proj_ref[:, sl]
        t = t * cs + pltpu.roll(t, DK_RET // 2, axis=1) * sn
        proj_ref[:, sl] = t * scale if hh < H_RET else t


def _head_cols(group, h):
    start = group * D_RET + h * DK_RET
    return slice(start, start + DK_RET)


def _gate_and_store(o, g, mix_ref, rows, h):
    o = _rms_rows(o)
    mix_ref[rows, _head_cols(0, h)] = (o * (g * (1.0 / (1.0 + jnp.exp(-g))))).astype(BF16)


def _short_conv(u, prev0, prev1, pos, w_ref):
    p1 = jnp.where(pos == 0, prev1, pltpu.roll(u, 1, axis=0))
    p2 = jnp.where(pos == 0, prev0, jnp.where(pos == 1, prev1, pltpu.roll(u, 2, axis=0)))
    return w_ref[0:1, :] * p2 + w_ref[1:2, :] * p1 + w_ref[2:3, :] * u


def _mixer_prompt_kernel(cd, x_ref, cos_ref, sin_ref, gmix_ref, w_in_ref, conv_w_ref, w_out_ref,
                         mask_ref, qd_ref, kd_ref,
                         out_ref, state_ref, conv_ref,
                         proj_ref, mix_ref):
    rows_total = x_ref.shape[0]

    @pl.when(pl.program_id(1) == 0)
    def _():
        state_ref[...] = jnp.zeros_like(state_ref)
        conv_ref[...] = jnp.zeros_like(conv_ref)

    x = x_ref[...]
    _project_and_rotate(x, gmix_ref, w_in_ref, cos_ref, sin_ref, proj_ref)

    for blk in range(rows_total // ROW_BLOCK):
        rows = slice(blk * ROW_BLOCK, (blk + 1) * ROW_BLOCK)
        for h in range(H_RET):
            q = proj_ref[rows, _head_cols(0, h)]
            k = proj_ref[rows, _head_cols(1, h)]
            vb = proj_ref[rows, _head_cols(2, h)].astype(BF16)
            a = lax.dot_general(q.astype(BF16), k.astype(BF16), (((1,), (1,)), ((), ())),
                                preferred_element_type=F32)
            s = state_ref[h]
            o = (jnp.dot((a * mask_ref[h]).astype(BF16), vb, preferred_element_type=F32)
                 + jnp.dot((q * qd_ref[h]).astype(BF16), s.astype(BF16), preferred_element_type=F32))
            kt = (k * kd_ref[h]).T.astype(BF16)
            state_ref[h] = cd[h] * s + jnp.dot(kt, vb, preferred_element_type=F32)
            _gate_and_store(o, proj_ref[rows, _head_cols(3, h)], mix_ref, rows, h)

    base = 4 * D_RET
    u = proj_ref[:, base + D_CONV:base + 2 * D_CONV] * proj_ref[:, base + 2 * D_CONV:base + 3 * D_CONV]
    pos = lax.broadcasted_iota(jnp.int32, u.shape, 0)
    y = _short_conv(u, conv_ref[0:1, :], conv_ref[1:2, :], pos, conv_w_ref)
    conv_ref[...] = u[rows_total - (CONV_W - 1):, :]
    mix_ref[:, D_RET:] = (proj_ref[:, base:base + D_CONV] * y).astype(BF16)

    out_ref[...] = x + jnp.dot(mix_ref[...], w_out_ref[...], preferred_element_type=F32)


def _mixer_sample_kernel(cd, seq, x_ref, cos_ref, sin_ref, gmix_ref, w_in_ref, conv_w_ref, w_out_ref,
                         mask_ref, qd_ref, kd_ref, state_in_ref, conv_in_ref, x_all_ref,
                         out_ref, state_ref, conv_ref,
                         proj_ref, mix_ref):
    del x_all_ref
    nseq = ROW_BLOCK // seq
    x = x_ref[...]
    _project_and_rotate(x, gmix_ref, w_in_ref, cos_ref, sin_ref, proj_ref)

    rows = slice(0, ROW_BLOCK)
    row_id = lax.broadcasted_iota(jnp.int32, (ROW_BLOCK, DK_RET), 0)
    for h in range(H_RET):
        q = proj_ref[rows, _head_cols(0, h)]
        k = proj_ref[rows, _head_cols(1, h)]
        vb = proj_ref[rows, _head_cols(2, h)].astype(BF16)
        a = lax.dot_general(q.astype(BF16), k.astype(BF16), (((1,), (1,)), ((), ())),
                            preferred_element_type=F32)
        qs = (q * qd_ref[h]).astype(BF16)
        ks = k * kd_ref[h]
        inter = []
        for b in range(nseq):
            s = state_in_ref[b, h]
            inter.append(jnp.dot(qs[b * seq:(b + 1) * seq, :], s.astype(BF16), preferred_element_type=F32))
            in_seq = (row_id >= b * seq) & (row_id < (b + 1) * seq)
            kt = jnp.where(in_seq, ks, 0.0).T.astype(BF16)
            state_ref[b, h] = cd[h] * s + jnp.dot(kt, vb, preferred_element_type=F32)
        o = (jnp.dot((a * mask_ref[h]).astype(BF16), vb, preferred_element_type=F32)
             + jnp.concatenate(inter, axis=0))
        _gate_and_store(o, proj_ref[rows, _head_cols(3, h)], mix_ref, rows, h)

    base = 4 * D_RET
    u = proj_ref[:, base + D_CONV:base + 2 * D_CONV] * proj_ref[:, base + 2 * D_CONV:base + 3 * D_CONV]
    pos = lax.broadcasted_iota(jnp.int32, u.shape, 0) % seq
    prev0 = jnp.concatenate([jnp.broadcast_to(conv_in_ref[b, 0:1, :], (seq, D_CONV)) for b in range(nseq)], axis=0)
    prev1 = jnp.concatenate([jnp.broadcast_to(conv_in_ref[b, 1:2, :], (seq, D_CONV)) for b in range(nseq)], axis=0)
    y = _short_conv(u, prev0, prev1, pos, conv_w_ref)
    for b in range(nseq):
        conv_ref[b] = u[(b + 1) * seq - (CONV_W - 1):(b + 1) * seq, :]
    mix_ref[:, D_RET:] = (proj_ref[:, base:base + D_CONV] * y).astype(BF16)

    out_ref[...] = x + jnp.dot(mix_ref[...], w_out_ref[...], preferred_element_type=F32)


def _full(shape):
    return pl.BlockSpec(shape, lambda *_: (0,) * len(shape))


def _mixer_prompt(x, extra_rows, gmix, w_in, conv_w, w_out):
    batch, seq, d = x.shape
    d_in = w_in.shape[1]
    rows = 512 if seq % 512 == 0 else ROW_BLOCK
    assert seq % rows == 0 and seq > CHUNK and ROW_BLOCK % CHUNK == 0
    tiles = seq // rows
    mask, qd, kd, cd = _prompt_tables()
    cos, sin = _rope_tables(jnp.arange(seq, dtype=jnp.int32))
    hshape = (H_RET, ROW_BLOCK, DK_RET)
    return pl.pallas_call(
        functools.partial(_mixer_prompt_kernel, cd),
        grid=(batch, tiles),
        in_specs=[
            pl.BlockSpec((None, rows, d), lambda b, l: (b, l, 0)),
            pl.BlockSpec((rows, DK_RET), lambda b, l: (l, 0)),
            pl.BlockSpec((rows, DK_RET), lambda b, l: (l, 0)),
            _full((1, d)), _full((d, d_in)), _full((CONV_W, D_CONV)), _full((d, d)),
            _full(hshape), _full(hshape), _full(hshape),
        ],
        out_specs=[
            pl.BlockSpec((rows, d), lambda b, l: (b * tiles + l, 0)),
            pl.BlockSpec((None, H_RET, DK_RET, DK_RET), lambda b, l: (b, 0, 0, 0)),
            pl.BlockSpec((None, CONV_W - 1, D_CONV), lambda b, l: (b, 0, 0)),
        ],
        out_shape=[
            jax.ShapeDtypeStruct((batch * seq + extra_rows, d), F32),
            jax.ShapeDtypeStruct((batch, H_RET, DK_RET, DK_RET), F32),
            jax.ShapeDtypeStruct((batch, CONV_W - 1, D_CONV), F32),
        ],
        scratch_shapes=[pltpu.VMEM((rows, d_in), F32), pltpu.VMEM((rows, d), BF16)],
        compiler_params=pltpu.CompilerParams(
            dimension_semantics=("parallel", "arbitrary"), vmem_limit_bytes=VMEM_LIMIT),
        name="mixer_prompt",
    )(x, cos, sin, gmix, w_in, conv_w, w_out, jnp.asarray(mask), jnp.asarray(qd), jnp.asarray(kd))


def _mixer_sample(x, x_all, state, conv_state, past_len, gmix, w_in, conv_w, w_out):
    batch, seq, d = x.shape
    d_in = w_in.shape[1]
    assert seq <= CHUNK and ROW_BLOCK % seq == 0 and seq >= CONV_W - 1
    nseq = ROW_BLOCK // seq
    assert batch % nseq == 0
    first_block, rem = divmod(x_all.shape[0] - batch * seq, ROW_BLOCK)
    assert rem == 0
    mask, qd, kd, cd = _sample_tables(seq)
    cos, sin = _rope_tables(past_len + jnp.tile(jnp.arange(seq, dtype=jnp.int32), nseq))
    hshape = (H_RET, ROW_BLOCK, DK_RET)
    x2 = x.reshape(batch * seq, d)
    return pl.pallas_call(
        functools.partial(_mixer_sample_kernel, cd, seq),
        grid=(batch // nseq,),
        in_specs=[
            pl.BlockSpec((ROW_BLOCK, d), lambda g: (g, 0)),
            _full((ROW_BLOCK, DK_RET)), _full((ROW_BLOCK, DK_RET)),
            _full((1, d)), _full((d, d_in)), _full((CONV_W, D_CONV)), _full((d, d)),
            _full(hshape), _full(hshape), _full(hshape),
            pl.BlockSpec((nseq, H_RET, DK_RET, DK_RET), lambda g: (g, 0, 0, 0)),
            pl.BlockSpec((nseq, CONV_W - 1, D_CONV), lambda g: (g, 0, 0)),
            pl.BlockSpec(memory_space=pl.ANY),
        ],
        out_specs=[
            pl.BlockSpec((ROW_BLOCK, d), lambda g: (first_block + g, 0)),
            pl.BlockSpec((nseq, H_RET, DK_RET, DK_RET), lambda g: (g, 0, 0, 0)),
            pl.BlockSpec((nseq, CONV_W - 1, D_CONV), lambda g: (g, 0, 0)),
        ],
        out_shape=[
            jax.ShapeDtypeStruct(x_all.shape, F32),
            jax.ShapeDtypeStruct((batch, H_RET, DK_RET, DK_RET), F32),
            jax.ShapeDtypeStruct((batch, CONV_W - 1, D_CONV), F32),
        ],
        input_output_aliases={12: 0},
        scratch_shapes=[pltpu.VMEM((ROW_BLOCK, d_in), F32), pltpu.VMEM((ROW_BLOCK, d), BF16)],
        compiler_params=pltpu.CompilerParams(
            dimension_semantics=("parallel",), vmem_limit_bytes=VMEM_LIMIT),
        name="mixer_sample",
    )(x2, cos, sin, gmix, w_in, conv_w, w_out, jnp.asarray(mask), jnp.asarray(qd), jnp.asarray(kd),
      state, conv_state, x_all)


def _sort16_pairs():
    def merge(lo, hi, r):
        step = r * 2
        if step < hi - lo:
            yield from merge(lo, hi, step)
            yield from merge(lo + r, hi, step)
            for i in range(lo + r, hi - r, step):
                yield (i, i + r)
        else:
            yield (lo, lo + r)

    def sort(lo, hi):
        if hi - lo >= 1:
            mid = lo + (hi - lo) // 2
            yield from sort(lo, mid)
            yield from sort(mid + 1, hi)
            yield from merge(lo, hi, 1)

    return tuple(sort(0, PEER_TOPK - 1))


_SORT16 = _sort16_pairs()


def _exchange(vals, i, j):
    a, b = vals[i], vals[j]
    if b is None:
        return
    if a is None:
        vals[i], vals[j] = b, None
        return
    vals[i], vals[j] = jnp.maximum(a, b), jnp.minimum(a, b)


def _sort16_desc(vals):
    vals = list(vals)
    for i, j in _SORT16:
        _exchange(vals, i, j)
    return vals


def _merge_top16(a, b):
    vals = []
    for k in range(PEER_TOPK):
        x, y = a[k], b[PEER_TOPK - 1 - k]
        vals.append(y if x is None else (x if y is None else jnp.maximum(x, y)))
    stride = PEER_TOPK // 2
    while stride:
        for i in range(PEER_TOPK):
            if i & stride == 0:
                _exchange(vals, i, i + stride)
        stride //= 2
    return vals


def _top16_of_rows(s):
    vals = _sort16_desc([s[r * SUBLANES:(r + 1) * SUBLANES, :] for r in range(N_KEYS // SUBLANES)])
    shift = SUBLANES // 2
    while shift:
        vals = _merge_top16(vals, [pltpu.roll(v, shift, axis=0) for v in vals])
        shift //= 2
    return vals


_REST_PAIRS = tuple((k1, k2) for k1 in range(2, PEER_TOPK) for k2 in range(1, PEER_TOPK)
                    if (k1 + 1) * (k2 + 1) <= PEER_TOPK)
assert len(_REST_PAIRS) <= PEER_TOPK


def _top16_sums(v1, v2):
    pad = lambda lst: lst + [None] * (PEER_TOPK - len(lst))
    row0 = [v1[0] + v2[k] for k in range(PEER_TOPK)]
    col0 = pad([v1[k] + v2[0] for k in range(1, PEER_TOPK)])
    row1 = pad([v1[1] + v2[k] for k in range(1, PEER_TOPK) if 2 * (k + 1) <= PEER_TOPK])
    rest = _sort16_desc(pad([v1[a] + v2[b] for a, b in _REST_PAIRS]))
    return _merge_top16(_merge_top16(row0, col0), _merge_top16(row1, rest))


def _count_leading(vals, pred):
    g8 = pred(vals[7])
    g4 = pred(jnp.where(g8, vals[11], vals[3]))
    g2 = pred(jnp.where(g8, jnp.where(g4, vals[13], vals[9]), jnp.where(g4, vals[5], vals[1])))
    quads = [jnp.where(g2, vals[4 * m + 2], vals[4 * m]) for m in range(4)]
    g1 = pred(jnp.where(g8, jnp.where(g4, quads[3], quads[2]), jnp.where(g4, quads[1], quads[0])))
    g16 = pred(vals[15])
    cnt = (jnp.where(g8, jnp.where(g4, 12.0, 8.0), jnp.where(g4, 4.0, 0.0))
           + jnp.where(g2, jnp.where(g1, 3.0, 2.0), jnp.where(g1, 1.0, 0.0)))
    return jnp.where(g16, 16.0, cnt)


def _pair_words(x):
    bits = pltpu.bitcast(x.astype(BF16).astype(F32), jnp.uint32)
    return bits | (bits >> 16)


def _route_lane_block(s1_ref, s2_ref, a_ref, c_ref, b_ref, r_ref, h, lanes):
    s1 = s1_ref[:, lanes]
    s2 = s2_ref[:, lanes]
    v1 = _top16_of_rows(s1)
    v2 = _top16_of_rows(s2)
    top = _top16_sums(v1, v2)
    tau = top[PEER_TOPK - 1]
    z = jnp.ones_like(tau)
    for k in range(1, PEER_TOPK):
        z = z + jnp.exp(top[k] - top[0])
    half_inv_z = 0.5 / z
    for r in range(N_KEYS // SUBLANES):
        rs = slice(r * SUBLANES, (r + 1) * SUBLANES)
        x1, x2 = s1[rs, :], s2[rs, :]
        cnt = _count_leading(v2, lambda v: x1 + v >= tau)
        rank = _count_leading(v2, lambda v: v > x2)
        a_ref[h, rs, lanes] = _pair_words(jnp.exp(x1 - v1[0]) * half_inv_z)
        c_ref[h, rs, lanes] = _pair_words(cnt)
        b_ref[rs, lanes] = jnp.exp(x2 - v2[0])
        r_ref[rs, lanes] = rank


_GELU_C0 = float(np.sqrt(2.0 / np.pi))
_GELU_C1 = float(np.sqrt(2.0 / np.pi) * 0.044715)


def _twice_gelu_tanh(x):
    return x * (1.0 + jnp.tanh(x * (_GELU_C0 + _GELU_C1 * (x * x))))


def _peer_kernel(rows_per_step, n_prompt_tiles, x_ref, gffn_ref, wq_ref, k1_ref, k2_ref, u_ref, vt_ref,
                 gfin_ref, yp_ref, ys_ref,
                 ht_ref, acc_ref, s1_ref, s2_ref, a_ref, c_ref, b_ref, r_ref, bb_ref, rb_ref,
                 act_ref, wact_ref):
    t = pl.program_id(0)
    e = pl.program_id(1)
    tokens = x_ref.shape[0]
    n_lane_blocks = tokens // LANES
    groups = N_KEYS // BF16_ROWS

    @pl.when(e == 0)
    def _route():
        h2 = _rms_rows(x_ref[...]) * gffn_ref[...]
        ht_ref[...] = h2.T.astype(BF16)
        acc_ref[...] = jnp.zeros_like(acc_ref)

        def head_body(h, carry):
            q0 = pl.multiple_of(h * 2 * D_KEY_HALF, 2 * D_KEY_HALF)
            qt = jnp.dot(wq_ref[pl.ds(q0, 2 * D_KEY_HALF), :], ht_ref[...], preferred_element_type=F32)
            s1_ref[...] = jnp.dot(k1_ref[h], qt[:D_KEY_HALF].astype(BF16), preferred_element_type=F32)
            s2_ref[...] = jnp.dot(k2_ref[h], qt[D_KEY_HALF:].astype(BF16), preferred_element_type=F32)

            def lane_body(lb, c2):
                lanes = pl.ds(pl.multiple_of(lb * LANES, LANES), LANES)
                _route_lane_block(s1_ref, s2_ref, a_ref, c_ref, b_ref, r_ref, h, lanes)
                return c2

            lax.fori_loop(0, n_lane_blocks, lane_body, 0)
            bb_ref[h] = b_ref[...].astype(BF16).reshape(groups, BF16_ROWS, tokens)
            rb_ref[h] = r_ref[...].astype(BF16).reshape(groups, BF16_ROWS, tokens)
            return carry

        lax.fori_loop(0, PEER_HEADS, head_body, 0)

    n_split = 2 if tokens % (2 * 2 * LANES) == 0 else 1
    width = tokens // n_split
    rows_half = rows_per_step // 2
    units = [(eh, sp) for eh in range(2) for sp in range(n_split)]
    row0 = pl.multiple_of(e * rows_per_step, SUBLANES)
    zero = jnp.zeros((), BF16)

    def row_as_bf16(words_ref, h, ii, lanes):
        tile = words_ref[h, pl.ds(row0, SUBLANES), lanes]
        return pltpu.bitcast(jnp.broadcast_to(tile[ii:ii + 1, :], (SUBLANES, LANES)), BF16)

    def gate_rows(ii, lanes):
        w = None
        for h in range(PEER_HEADS):
            c16 = row_as_bf16(c_ref, h, ii, lanes)
            a16 = row_as_bf16(a_ref, h, ii, lanes)
            term = jnp.where(rb_ref[h, :, :, lanes] < c16[None], bb_ref[h, :, :, lanes], zero) * a16[None]
            w = term if w is None else w + term
        rs = slice(ii * N_KEYS, (ii + 1) * N_KEYS)
        act = _twice_gelu_tanh(act_ref[rs, lanes]).reshape(groups, BF16_ROWS, LANES)
        wact_ref[rs, lanes] = (w * act).reshape(N_KEYS, LANES)

    for eh, sp in units:
        cols = slice(sp * width, (sp + 1) * width)
        rs = slice(eh * rows_half * N_KEYS, (eh + 1) * rows_half * N_KEYS)
        act_ref[rs, cols] = jnp.dot(u_ref[rs, :], ht_ref[:, cols], preferred_element_type=F32).astype(BF16)
    for eh, sp in units:
        for lb in range(width // LANES):
            lane0 = sp * width + lb * LANES
            for ii in range(eh * rows_half, (eh + 1) * rows_half):
                gate_rows(ii, slice(lane0, lane0 + LANES))
    for eh, sp in units:
        cols = slice(sp * width, (sp + 1) * width)
        rs = slice(eh * rows_half * N_KEYS, (eh + 1) * rows_half * N_KEYS)
        acc_ref[:, cols] += jnp.dot(vt_ref[:, rs], wact_ref[rs, cols], preferred_element_type=F32)

    @pl.when(e == pl.num_programs(1) - 1)
    def _():
        out = x_ref[...] + acc_ref[...].T
        y = _rms_rows(out) * gfin_ref[...]

        @pl.when(t < n_prompt_tiles)
        def _():
            yp_ref[...] = y

        @pl.when(t >= n_prompt_tiles)
        def _():
            ys_ref[...] = y


def _peer(x, n_prompt_rows, gffn, wq_t, keys1, keys2, u_tab, v_tab_t, gfin):
    total, d = x.shape
    n_experts = u_tab.shape[0]
    n_sample_rows = total - n_prompt_rows
    tokens = next(t for t in (512, 256, 128) if n_prompt_rows % t == 0 and n_sample_rows % t == 0)
    n_prompt_tiles = n_prompt_rows // tokens
    rows_per_step = SUBLANES
    experts_per_step = rows_per_step * N_KEYS
    groups = N_KEYS // BF16_ROWS
    head_shape = (PEER_HEADS, N_KEYS, tokens)
    return pl.pallas_call(
        functools.partial(_peer_kernel, rows_per_step, n_prompt_tiles),
        grid=(total // tokens, n_experts // experts_per_step),
        in_specs=[
            pl.BlockSpec((tokens, d), lambda t, e: (t, 0)),
            _full((1, d)),
            _full(wq_t.shape), _full(keys1.shape), _full(keys2.shape),
            pl.BlockSpec((experts_per_step, d), lambda t, e: (e, 0)),
            pl.BlockSpec((d, experts_per_step), lambda t, e: (0, e)),
            _full((1, d)),
        ],
        out_specs=[
            pl.BlockSpec((tokens, d), lambda t, e: (jnp.minimum(t, n_prompt_tiles - 1), 0)),
            pl.BlockSpec((tokens, d), lambda t, e: (jnp.maximum(t - n_prompt_tiles, 0), 0)),
        ],
        out_shape=[
            jax.ShapeDtypeStruct((n_prompt_rows, d), F32),
            jax.ShapeDtypeStruct((n_sample_rows, d), F32),
        ],
        scratch_shapes=[
            pltpu.VMEM((d, tokens), BF16),
            pltpu.VMEM((d, tokens), F32),
            pltpu.VMEM((N_KEYS, tokens), F32), pltpu.VMEM((N_KEYS, tokens), F32),
            pltpu.VMEM(head_shape, jnp.uint32), pltpu.VMEM(head_shape, jnp.uint32),
            pltpu.VMEM((N_KEYS, tokens), F32), pltpu.VMEM((N_KEYS, tokens), F32),
            pltpu.VMEM((PEER_HEADS, groups, BF16_ROWS, tokens), BF16),
            pltpu.VMEM((PEER_HEADS, groups, BF16_ROWS, tokens), BF16),
            pltpu.VMEM((experts_per_step, tokens), BF16),
            pltpu.VMEM((experts_per_step, tokens), BF16),
        ],
        compiler_params=pltpu.CompilerParams(
            dimension_semantics=("arbitrary", "arbitrary"), vmem_limit_bytes=VMEM_LIMIT),
        name="peer",
    )(x, gffn, wq_t, keys1, keys2, u_tab, v_tab_t, gfin)


def kernel(x_prompt, x_sample, state_ret, state_conv, norm_mix_g, w_in, conv_w, w_out, norm_ffn_g,
           peer_wq, peer_keys1, peer_keys2, peer_u, peer_v, norm_final_g):
    depth = w_in.shape[0]
    assert depth == 1
    batch, seq, d = x_prompt.shape
    dec_batch, dec_seq, _ = x_sample.shape
    n_prompt_rows, n_sample_rows = batch * seq, dec_batch * dec_seq

    gmix = norm_mix_g[0][None, :]
    gffn = norm_ffn_g[0][None, :]
    gfin = norm_final_g[None, :]
    w_in_b = w_in[0].astype(BF16)
    w_out_b = w_out[0].astype(BF16)
    wq_t = peer_wq[0].T.astype(BF16)
    keys1 = peer_keys1[0].astype(BF16)
    keys2 = peer_keys2[0].astype(BF16)
    u_tab = peer_u[0].astype(BF16)
    v_tab_t = peer_v[0].T.astype(BF16)

    x_all, ret_p, conv_p = _mixer_prompt(x_prompt, n_sample_rows, gmix, w_in_b, conv_w[0], w_out_b)
    x_all, ret_s, conv_s = _mixer_sample(x_sample, x_all, state_ret[0], state_conv[0], PAST_LEN,
                                         gmix, w_in_b, conv_w[0], w_out_b)
    y_prompt, y_sample = _peer(x_all, n_prompt_rows, gffn, wq_t, keys1, keys2, u_tab, v_tab_t, gfin)
    return (y_prompt.reshape(batch, seq, d), y_sample.reshape(dec_batch, dec_seq, d),
            ret_p[None], conv_p[None], ret_s[None], conv_s[None])
```

```python
import functools

import numpy as np
import jax
import jax.numpy as jnp
from jax import lax
from jax.experimental import pallas as pl
from jax.experimental.pallas import tpu as pltpu

F32 = jnp.float32
BF16 = jnp.bfloat16

H_RET = 4
DK_RET = 128
D_RET = H_RET * DK_RET
D_CONV = 512
CONV_W = 3
CHUNK = 64
ROPE_THETA = 10000.0
RMS_EPS = 1e-6
PEER_HEADS = 8
N_KEYS = 128
PEER_TOPK = 16
D_KEY_HALF = 128
PAST_LEN = 4096

ROW_BLOCK = 128
LANES = 128
SUBLANES = 8
BF16_ROWS = 16
VMEM_LIMIT = 56 * 1024 * 1024
ROWS_PER_STEP = 16


def _log_decay():
    return np.log1p(-np.exp2(-5.0 - np.arange(H_RET))).astype(np.float32).astype(np.float64)


def _retention_tables(chunk):
    lg = _log_decay()[:, None, None]
    idx = np.arange(ROW_BLOCK)
    ci, cj = idx[:, None] // chunk, idx[None, :] // chunk
    dist = (idx[:, None] - idx[None, :]).astype(np.float64)
    same = np.exp(lg * np.abs(dist)[None])
    later = np.exp(lg * dist[None])
    mask = np.where((ci == cj)[None], same, np.where((ci > cj)[None], later, 0.0))
    return mask.astype(np.float32)


def _prompt_tables():
    lg = _log_decay()
    idx = np.arange(ROW_BLOCK, dtype=np.float64)
    mask = _retention_tables(CHUNK)
    qd = np.exp(lg[:, None] * (idx[None] + 1.0))
    kd = np.exp(lg[:, None] * (ROW_BLOCK - 1.0 - idx[None]))
    cd = np.exp(lg * ROW_BLOCK)
    bc = lambda t: np.broadcast_to(t[:, :, None], (H_RET, ROW_BLOCK, DK_RET)).astype(np.float32)
    return mask, bc(qd), bc(kd), [float(c) for c in cd]


def _sample_tables(seq):
    lg = _log_decay()
    idx = np.arange(ROW_BLOCK)
    pos = (idx % seq).astype(np.float64)
    same_seq = (idx[:, None] // seq) == (idx[None, :] // seq)
    dist = np.abs(pos[:, None] - pos[None, :])
    mask = np.where(same_seq[None], np.exp(lg[:, None, None] * dist[None]), 0.0).astype(np.float32)
    qd = np.exp(lg[:, None] * (pos[None] + 1.0))
    kd = np.exp(lg[:, None] * (seq - 1.0 - pos[None]))
    cd = np.exp(lg * seq)
    bc = lambda t: np.broadcast_to(t[:, :, None], (H_RET, ROW_BLOCK, DK_RET)).astype(np.float32)
    return mask, bc(qd), bc(kd), [float(c) for c in cd]


def _rope_tables(pos):
    half = DK_RET // 2
    inv = ROPE_THETA ** (-jnp.arange(half, dtype=F32) / half)
    ang = pos.astype(F32)[:, None] * inv[None, :]
    cos, sin = jnp.cos(ang), jnp.sin(ang)
    return jnp.concatenate([cos, cos], axis=1), jnp.concatenate([-sin, sin], axis=1)


def _rms_rows(x):
    return x * lax.rsqrt(jnp.mean(x * x, axis=-1, keepdims=True) + RMS_EPS)


def _project_and_rotate(x, gmix_ref, w_in_ref, cos_ref, sin_ref, proj_ref):
    h = (_rms_rows(x) * gmix_ref[...]).astype(BF16)
    proj_ref[...] = jnp.dot(h, w_in_ref[...], preferred_element_type=F32)
    cs, sn = cos_ref[...], sin_ref[...]
    scale = DK_RET ** -0.5
    for hh in range(2 * H_RET):
        sl = slice(hh * DK_RET, (hh + 1) * DK_RET)
        t = proj_ref[:, sl]
        t = t * cs + pltpu.roll(t, DK_RET // 2, axis=1) * sn
        proj_ref[:, sl] = t * scale if hh < H_RET else t


def _head_cols(group, h):
    start = group * D_RET + h * DK_RET
    return slice(start, start + DK_RET)


def _gate_and_store(o, g, mix_ref, rows, h):
    o = _rms_rows(o)
    mix_ref[rows, _head_cols(0, h)] = (o * (g * (1.0 / (1.0 + jnp.exp(-g))))).astype(BF16)


def _short_conv(u, prev0, prev1, pos, w_ref):
    p1 = jnp.where(pos == 0, prev1, pltpu.roll(u, 1, axis=0))
    p2 = jnp.where(pos == 0, prev0, jnp.where(pos == 1, prev1, pltpu.roll(u, 2, axis=0)))
    return w_ref[0:1, :] * p2 + w_ref[1:2, :] * p1 + w_ref[2:3, :] * u


def _mixer_prompt_kernel(cd, x_ref, cos_ref, sin_ref, gmix_ref, w_in_ref, conv_w_ref, w_out_ref,
                         mask_ref, qd_ref, kd_ref,
                         out_ref, state_ref, conv_ref,
                         proj_ref, mix_ref):
    rows_total = x_ref.shape[0]

    @pl.when(pl.program_id(1) == 0)
    def _():
        state_ref[...] = jnp.zeros_like(state_ref)
        conv_ref[...] = jnp.zeros_like(conv_ref)

    x = x_ref[...]
    _project_and_rotate(x, gmix_ref, w_in_ref, cos_ref, sin_ref, proj_ref)

    for blk in range(rows_total // ROW_BLOCK):
        rows = slice(blk * ROW_BLOCK, (blk + 1) * ROW_BLOCK)
        for h in range(H_RET):
            q = proj_ref[rows, _head_cols(0, h)]
            k = proj_ref[rows, _head_cols(1, h)]
            vb = proj_ref[rows, _head_cols(2, h)].astype(BF16)
            a = lax.dot_general(q.astype(BF16), k.astype(BF16), (((1,), (1,)), ((), ())),
                                preferred_element_type=F32)
            s = state_ref[h]
            o = (jnp.dot((a * mask_ref[h]).astype(BF16), vb, preferred_element_type=F32)
                 + jnp.dot((q * qd_ref[h]).astype(BF16), s.astype(BF16), preferred_element_type=F32))
            kt = (k * kd_ref[h]).T.astype(BF16)
            state_ref[h] = cd[h] * s + jnp.dot(kt, vb, preferred_element_type=F32)
            _gate_and_store(o, proj_ref[rows, _head_cols(3, h)], mix_ref, rows, h)

    base = 4 * D_RET
    u = proj_ref[:, base + D_CONV:base + 2 * D_CONV] * proj_ref[:, base + 2 * D_CONV:base + 3 * D_CONV]
    pos = lax.broadcasted_iota(jnp.int32, u.shape, 0)
    y = _short_conv(u, conv_ref[0:1, :], conv_ref[1:2, :], pos, conv_w_ref)
    conv_ref[...] = u[rows_total - (CONV_W - 1):, :]
    mix_ref[:, D_RET:] = (proj_ref[:, base:base + D_CONV] * y).astype(BF16)

    out_ref[...] = x + jnp.dot(mix_ref[...], w_out_ref[...], preferred_element_type=F32)


def _mixer_sample_kernel(cd, seq, x_ref, cos_ref, sin_ref, gmix_ref, w_in_ref, conv_w_ref, w_out_ref,
                         mask_ref, qd_ref, kd_ref, state_in_ref, conv_in_ref, x_all_ref,
                         out_ref, state_ref, conv_ref,
                         proj_ref, mix_ref):
    del x_all_ref
    nseq = ROW_BLOCK // seq
    x = x_ref[...]
    _project_and_rotate(x, gmix_ref, w_in_ref, cos_ref, sin_ref, proj_ref)

    rows = slice(0, ROW_BLOCK)
    row_id = lax.broadcasted_iota(jnp.int32, (ROW_BLOCK, DK_RET), 0)
    for h in range(H_RET):
        q = proj_ref[rows, _head_cols(0, h)]
        k = proj_ref[rows, _head_cols(1, h)]
        vb = proj_ref[rows, _head_cols(2, h)].astype(BF16)
        a = lax.dot_general(q.astype(BF16), k.astype(BF16), (((1,), (1,)), ((), ())),
                            preferred_element_type=F32)
        qs = (q * qd_ref[h]).astype(BF16)
        ks = k * kd_ref[h]
        inter = []
        for b in range(nseq):
            s = state_in_ref[b, h]
            inter.append(jnp.dot(qs[b * seq:(b + 1) * seq, :], s.astype(BF16), preferred_element_type=F32))
            in_seq = (row_id >= b * seq) & (row_id < (b + 1) * seq)
            kt = jnp.where(in_seq, ks, 0.0).T.astype(BF16)
            state_ref[b, h] = cd[h] * s + jnp.dot(kt, vb, preferred_element_type=F32)
        o = (jnp.dot((a * mask_ref[h]).astype(BF16), vb, preferred_element_type=F32)
             + jnp.concatenate(inter, axis=0))
        _gate_and_store(o, proj_ref[rows, _head_cols(3, h)], mix_ref, rows, h)

    base = 4 * D_RET
    u = proj_ref[:, base + D_CONV:base + 2 * D_CONV] * proj_ref[:, base + 2 * D_CONV:base + 3 * D_CONV]
    pos = lax.broadcasted_iota(jnp.int32, u.shape, 0) % seq
    prev0 = jnp.concatenate([jnp.broadcast_to(conv_in_ref[b, 0:1, :], (seq, D_CONV)) for b in range(nseq)], axis=0)
    prev1 = jnp.concatenate([jnp.broadcast_to(conv_in_ref[b, 1:2, :], (seq, D_CONV)) for b in range(nseq)], axis=0)
    y = _short_conv(u, prev0, prev1, pos, conv_w_ref)
    for b in range(nseq):
        conv_ref[b] = u[(b + 1) * seq - (CONV_W - 1):(b + 1) * seq, :]
    mix_ref[:, D_RET:] = (proj_ref[:, base:base + D_CONV] * y).astype(BF16)

    out_ref[...] = x + jnp.dot(mix_ref[...], w_out_ref[...], preferred_element_type=F32)


def _full(shape):
    return pl.BlockSpec(shape, lambda *_: (0,) * len(shape))


def _mixer_prompt(x, extra_rows, gmix, w_in, conv_w, w_out):
    batch, seq, d = x.shape
    d_in = w_in.shape[1]
    rows = 512 if seq % 512 == 0 else ROW_BLOCK
    assert seq % rows == 0 and seq > CHUNK and ROW_BLOCK % CHUNK == 0
    tiles = seq // rows
    mask, qd, kd, cd = _prompt_tables()
    cos, sin = _rope_tables(jnp.arange(seq, dtype=jnp.int32))
    hshape = (H_RET, ROW_BLOCK, DK_RET)
    return pl.pallas_call(
        functools.partial(_mixer_prompt_kernel, cd),
        grid=(batch, tiles),
        in_specs=[
            pl.BlockSpec((None, rows, d), lambda b, l: (b, l, 0)),
            pl.BlockSpec((rows, DK_RET), lambda b, l: (l, 0)),
            pl.BlockSpec((rows, DK_RET), lambda b, l: (l, 0)),
            _full((1, d)), _full((d, d_in)), _full((CONV_W, D_CONV)), _full((d, d)),
            _full(hshape), _full(hshape), _full(hshape),
        ],
        out_specs=[
            pl.BlockSpec((rows, d), lambda b, l: (b * tiles + l, 0)),
            pl.BlockSpec((None, H_RET, DK_RET, DK_RET), lambda b, l: (b, 0, 0, 0)),
            pl.BlockSpec((None, CONV_W - 1, D_CONV), lambda b, l: (b, 0, 0)),
        ],
        out_shape=[
            jax.ShapeDtypeStruct((batch * seq + extra_rows, d), F32),
            jax.ShapeDtypeStruct((batch, H_RET, DK_RET, DK_RET), F32),
            jax.ShapeDtypeStruct((batch, CONV_W - 1, D_CONV), F32),
        ],
        scratch_shapes=[pltpu.VMEM((rows, d_in), F32), pltpu.VMEM((rows, d), BF16)],
        compiler_params=pltpu.CompilerParams(
            dimension_semantics=("parallel", "arbitrary"), vmem_limit_bytes=VMEM_LIMIT),
        name="mixer_prompt",
    )(x, cos, sin, gmix, w_in, conv_w, w_out, jnp.asarray(mask), jnp.asarray(qd), jnp.asarray(kd))


def _mixer_sample(x, x_all, state, conv_state, past_len, gmix, w_in, conv_w, w_out):
    batch, seq, d = x.shape
    d_in = w_in.shape[1]
    assert seq <= CHUNK and ROW_BLOCK % seq == 0 and seq >= CONV_W - 1
    nseq = ROW_BLOCK // seq
    assert batch % nseq == 0
    first_block, rem = divmod(x_all.shape[0] - batch * seq, ROW_BLOCK)
    assert rem == 0
    mask, qd, kd, cd = _sample_tables(seq)
    cos, sin = _rope_tables(past_len + jnp.tile(jnp.arange(seq, dtype=jnp.int32), nseq))
    hshape = (H_RET, ROW_BLOCK, DK_RET)
    x2 = x.reshape(batch * seq, d)
    return pl.pallas_call(
        functools.partial(_mixer_sample_kernel, cd, seq),
        grid=(batch // nseq,),
        in_specs=[
            pl.BlockSpec((ROW_BLOCK, d), lambda g: (g, 0)),
            _full((ROW_BLOCK, DK_RET)), _full((ROW_BLOCK, DK_RET)),
            _full((1, d)), _full((d, d_in)), _full((CONV_W, D_CONV)), _full((d, d)),
            _full(hshape), _full(hshape), _full(hshape),
            pl.BlockSpec((nseq, H_RET, DK_RET, DK_RET), lambda g: (g, 0, 0, 0)),
            pl.BlockSpec((nseq, CONV_W - 1, D_CONV), lambda g: (g, 0, 0)),
            pl.BlockSpec(memory_space=pl.ANY),
        ],
        out_specs=[
            pl.BlockSpec((ROW_BLOCK, d), lambda g: (first_block + g, 0)),
            pl.BlockSpec((nseq, H_RET, DK_RET, DK_RET), lambda g: (g, 0, 0, 0)),
            pl.BlockSpec((nseq, CONV_W - 1, D_CONV), lambda g: (g, 0, 0)),
        ],
        out_shape=[
            jax.ShapeDtypeStruct(x_all.shape, F32),
            jax.ShapeDtypeStruct((batch, H_RET, DK_RET, DK_RET), F32),
            jax.ShapeDtypeStruct((batch, CONV_W - 1, D_CONV), F32),
        ],
        input_output_aliases={12: 0},
        scratch_shapes=[pltpu.VMEM((ROW_BLOCK, d_in), F32), pltpu.VMEM((ROW_BLOCK, d), BF16)],
        compiler_params=pltpu.CompilerParams(
            dimension_semantics=("parallel",), vmem_limit_bytes=VMEM_LIMIT),
        name="mixer_sample",
    )(x2, cos, sin, gmix, w_in, conv_w, w_out, jnp.asarray(mask), jnp.asarray(qd), jnp.asarray(kd),
      state, conv_state, x_all)


def _sort16_pairs():
    def merge(lo, hi, r):
        step = r * 2
        if step < hi - lo:
            yield from merge(lo, hi, step)
            yield from merge(lo + r, hi, step)
            for i in range(lo + r, hi - r, step):
                yield (i, i + r)
        else:
            yield (lo, lo + r)

    def sort(lo, hi):
        if hi - lo >= 1:
            mid = lo + (hi - lo) // 2
            yield from sort(lo, mid)
            yield from sort(mid + 1, hi)
            yield from merge(lo, hi, 1)

    return tuple(sort(0, PEER_TOPK - 1))


_SORT16 = _sort16_pairs()


def _exchange(vals, i, j):
    a, b = vals[i], vals[j]
    if b is None:
        return
    if a is None:
        vals[i], vals[j] = b, None
        return
    vals[i], vals[j] = jnp.maximum(a, b), jnp.minimum(a, b)


def _sort16_desc(vals):
    vals = list(vals)
    for i, j in _SORT16:
        _exchange(vals, i, j)
    return vals


def _merge_top16(a, b):
    vals = []
    for k in range(PEER_TOPK):
        x, y = a[k], b[PEER_TOPK - 1 - k]
        vals.append(y if x is None else (x if y is None else jnp.maximum(x, y)))
    stride = PEER_TOPK // 2
    while stride:
        for i in range(PEER_TOPK):
            if i & stride == 0:
                _exchange(vals, i, i + stride)
        stride //= 2
    return vals


def _top16_of_rows(s):
    vals = _sort16_desc([s[r * SUBLANES:(r + 1) * SUBLANES, :] for r in range(N_KEYS // SUBLANES)])
    shift = SUBLANES // 2
    while shift:
        vals = _merge_top16(vals, [pltpu.roll(v, shift, axis=0) for v in vals])
        shift //= 2
    return vals


_REST_PAIRS = tuple((k1, k2) for k1 in range(2, PEER_TOPK) for k2 in range(1, PEER_TOPK)
                    if (k1 + 1) * (k2 + 1) <= PEER_TOPK)
assert len(_REST_PAIRS) <= PEER_TOPK


def _top16_sums(v1, v2):
    pad = lambda lst: lst + [None] * (PEER_TOPK - len(lst))
    row0 = [v1[0] + v2[k] for k in range(PEER_TOPK)]
    col0 = pad([v1[k] + v2[0] for k in range(1, PEER_TOPK)])
    row1 = pad([v1[1] + v2[k] for k in range(1, PEER_TOPK) if 2 * (k + 1) <= PEER_TOPK])
    rest = _sort16_desc(pad([v1[a] + v2[b] for a, b in _REST_PAIRS]))
    return _merge_top16(_merge_top16(row0, col0), _merge_top16(row1, rest))


def _count_leading(vals, pred):
    g8 = pred(vals[7])
    g4 = pred(jnp.where(g8, vals[11], vals[3]))
    g2 = pred(jnp.where(g8, jnp.where(g4, vals[13], vals[9]), jnp.where(g4, vals[5], vals[1])))
    quads = [jnp.where(g2, vals[4 * m + 2], vals[4 * m]) for m in range(4)]
    g1 = pred(jnp.where(g8, jnp.where(g4, quads[3], quads[2]), jnp.where(g4, quads[1], quads[0])))
    g16 = pred(vals[15])
    cnt = (jnp.where(g8, jnp.where(g4, 12.0, 8.0), jnp.where(g4, 4.0, 0.0))
           + jnp.where(g2, jnp.where(g1, 3.0, 2.0), jnp.where(g1, 1.0, 0.0)))
    return jnp.where(g16, 16.0, cnt)


def _pair_words(x):
    bits = pltpu.bitcast(x.astype(BF16).astype(F32), jnp.uint32)
    return bits | (bits >> 16)


def _route_lane_block(s1_ref, s2_ref, a_ref, c_ref, b_ref, r_ref, h, lanes):
    s1 = s1_ref[:, lanes]
    s2 = s2_ref[:, lanes]
    v1 = _top16_of_rows(s1)
    v2 = _top16_of_rows(s2)
    top = _top16_sums(v1, v2)
    tau = top[PEER_TOPK - 1]
    z = jnp.ones_like(tau)
    for k in range(1, PEER_TOPK):
        z = z + jnp.exp(top[k] - top[0])
    half_inv_z = 0.5 / z
    for r in range(N_KEYS // SUBLANES):
        rs = slice(r * SUBLANES, (r + 1) * SUBLANES)
        x1, x2 = s1[rs, :], s2[rs, :]
        cnt = _count_leading(v2, lambda v: x1 + v >= tau)
        rank = _count_leading(v2, lambda v: v > x2)
        a_ref[h, rs, lanes] = _pair_words(jnp.exp(x1 - v1[0]) * half_inv_z)
        c_ref[h, rs, lanes] = _pair_words(cnt)
        b_ref[rs, lanes] = jnp.exp(x2 - v2[0])
        r_ref[rs, lanes] = rank


_GELU_C0 = float(np.sqrt(2.0 / np.pi))
_GELU_C1 = float(np.sqrt(2.0 / np.pi) * 0.044715)


def _twice_gelu_tanh(x):
    return x * (1.0 + jnp.tanh(x * (_GELU_C0 + _GELU_C1 * (x * x))))


def _peer_kernel(rows_per_step, n_prompt_tiles, x_ref, gffn_ref, wq_ref, k1_ref, k2_ref, u_ref, vt_ref,
                 gfin_ref, yp_ref, ys_ref,
                 ht_ref, acc_ref, s1_ref, s2_ref, a_ref, c_ref, b_ref, r_ref, bb_ref, rb_ref,
                 act_ref, wact_ref):
    t = pl.program_id(0)
    e = pl.program_id(1)
    tokens = x_ref.shape[0]
    n_lane_blocks = tokens // LANES
    groups = N_KEYS // BF16_ROWS

    @pl.when(e == 0)
    def _route():
        h2 = _rms_rows(x_ref[...]) * gffn_ref[...]
        ht_ref[...] = h2.T.astype(BF16)
        acc_ref[...] = jnp.zeros_like(acc_ref)

        def head_body(h, carry):
            q0 = pl.multiple_of(h * 2 * D_KEY_HALF, 2 * D_KEY_HALF)
            qt = jnp.dot(wq_ref[pl.ds(q0, 2 * D_KEY_HALF), :], ht_ref[...], preferred_element_type=F32)
            s1_ref[...] = jnp.dot(k1_ref[h], qt[:D_KEY_HALF].astype(BF16), preferred_element_type=F32)
            s2_ref[...] = jnp.dot(k2_ref[h], qt[D_KEY_HALF:].astype(BF16), preferred_element_type=F32)

            def lane_body(lb, c2):
                lanes = pl.ds(pl.multiple_of(lb * LANES, LANES), LANES)
                _route_lane_block(s1_ref, s2_ref, a_ref, c_ref, b_ref, r_ref, h, lanes)
                return c2

            lax.fori_loop(0, n_lane_blocks, lane_body, 0)
            bb_ref[h] = b_ref[...].astype(BF16).reshape(groups, BF16_ROWS, tokens)
            rb_ref[h] = r_ref[...].astype(BF16).reshape(groups, BF16_ROWS, tokens)
            return carry

        lax.fori_loop(0, PEER_HEADS, head_body, 0)

    n_split = 2 if tokens % (2 * 2 * LANES) == 0 else 1
    width = tokens // n_split
    rows_half = rows_per_step // 2
    units = [(eh, sp) for eh in range(2) for sp in range(n_split)]
    row0 = pl.multiple_of(e * rows_per_step, SUBLANES)
    zero = jnp.zeros((), BF16)

    def row_as_bf16(words_ref, h, ii, lanes):
        tile = words_ref[h, pl.ds(row0 + ii // SUBLANES * SUBLANES, SUBLANES), lanes]
        sub = ii % SUBLANES
        return pltpu.bitcast(jnp.broadcast_to(tile[sub:sub + 1, :], (SUBLANES, LANES)), BF16)

    def gate_rows(ii, lanes):
        w = None
        for h in range(PEER_HEADS):
            c16 = row_as_bf16(c_ref, h, ii, lanes)
            a16 = row_as_bf16(a_ref, h, ii, lanes)
            term = jnp.where(rb_ref[h, :, :, lanes] < c16[None], bb_ref[h, :, :, lanes], zero) * a16[None]
            w = term if w is None else w + term
        rs = slice(ii * N_KEYS, (ii + 1) * N_KEYS)
        act = _twice_gelu_tanh(act_ref[rs, lanes]).reshape(groups, BF16_ROWS, LANES)
        wact_ref[rs, lanes] = (w * act).reshape(N_KEYS, LANES)

    for eh, sp in units:
        cols = slice(sp * width, (sp + 1) * width)
        rs = slice(eh * rows_half * N_KEYS, (eh + 1) * rows_half * N_KEYS)
        act_ref[rs, cols] = jnp.dot(u_ref[rs, :], ht_ref[:, cols], preferred_element_type=F32).astype(BF16)
    for eh, sp in units:
        for lb in range(width // LANES):
            lane0 = sp * width + lb * LANES
            for ii in range(eh * rows_half, (eh + 1) * rows_half):
                gate_rows(ii, slice(lane0, lane0 + LANES))
    for eh, sp in units:
        cols = slice(sp * width, (sp + 1) * width)
        rs = slice(eh * rows_half * N_KEYS, (eh + 1) * rows_half * N_KEYS)
        acc_ref[:, cols] += jnp.dot(vt_ref[:, rs], wact_ref[rs, cols], preferred_element_type=F32)

    @pl.when(e == pl.num_programs(1) - 1)
    def _():
        out = x_ref[...] + acc_ref[...].T
        y = _rms_rows(out) * gfin_ref[...]

        @pl.when(t < n_prompt_tiles)
        def _():
            yp_ref[...] = y

        @pl.when(t >= n_prompt_tiles)
        def _():
            ys_ref[...] = y


def _peer(x, n_prompt_rows, gffn, wq_t, keys1, keys2, u_tab, v_tab_t, gfin):
    total, d = x.shape
    n_experts = u_tab.shape[0]
    n_sample_rows = total - n_prompt_rows
    tokens = next(t for t in (512, 256, 128) if n_prompt_rows % t == 0 and n_sample_rows % t == 0)
    n_prompt_tiles = n_prompt_rows // tokens
    rows_per_step = ROWS_PER_STEP
    experts_per_step = rows_per_step * N_KEYS
    groups = N_KEYS // BF16_ROWS
    head_shape = (PEER_HEADS, N_KEYS, tokens)
    return pl.pallas_call(
        functools.partial(_peer_kernel, rows_per_step, n_prompt_tiles),
        grid=(total // tokens, n_experts // experts_per_step),
        in_specs=[
            pl.BlockSpec((tokens, d), lambda t, e: (t, 0)),
            _full((1, d)),
            _full(wq_t.shape), _full(keys1.shape), _full(keys2.shape),
            pl.BlockSpec((experts_per_step, d), lambda t, e: (e, 0)),
            pl.BlockSpec((d, experts_per_step), lambda t, e: (0, e)),
            _full((1, d)),
        ],
        out_specs=[
            pl.BlockSpec((tokens, d), lambda t, e: (jnp.minimum(t, n_prompt_tiles - 1), 0)),
            pl.BlockSpec((tokens, d), lambda t, e: (jnp.maximum(t - n_prompt_tiles, 0), 0)),
        ],
        out_shape=[
            jax.ShapeDtypeStruct((n_prompt_rows, d), F32),
            jax.ShapeDtypeStruct((n_sample_rows, d), F32),
        ],
        scratch_shapes=[
            pltpu.VMEM((d, tokens), BF16),
            pltpu.VMEM((d, tokens), F32),
            pltpu.VMEM((N_KEYS, tokens), F32), pltpu.VMEM((N_KEYS, tokens), F32),
            pltpu.VMEM(head_shape, jnp.uint32), pltpu.VMEM(head_shape, jnp.uint32),
            pltpu.VMEM((N_KEYS, tokens), F32), pltpu.VMEM((N_KEYS, tokens), F32),
            pltpu.VMEM((PEER_HEADS, groups, BF16_ROWS, tokens), BF16),
            pltpu.VMEM((PEER_HEADS, groups, BF16_ROWS, tokens), BF16),
            pltpu.VMEM((experts_per_step, tokens), BF16),
            pltpu.VMEM((experts_per_step, tokens), BF16),
        ],
        compiler_params=pltpu.CompilerParams(
            dimension_semantics=("arbitrary", "arbitrary"), vmem_limit_bytes=VMEM_LIMIT),
        name="peer",
    )(x, gffn, wq_t, keys1, keys2, u_tab, v_tab_t, gfin)


def kernel(x_prompt, x_sample, state_ret, state_conv, norm_mix_g, w_in, conv_w, w_out, norm_ffn_g,
           peer_wq, peer_keys1, peer_keys2, peer_u, peer_v, norm_final_g):
    depth = w_in.shape[0]
    assert depth == 1
    batch, seq, d = x_prompt.shape
    dec_batch, dec_seq, _ = x_sample.shape
    n_prompt_rows, n_sample_rows = batch * seq, dec_batch * dec_seq

    gmix = norm_mix_g[0][None, :]
    gffn = norm_ffn_g[0][None, :]
    gfin = norm_final_g[None, :]
    w_in_b = w_in[0].astype(BF16)
    w_out_b = w_out[0].astype(BF16)
    wq_t = peer_wq[0].T.astype(BF16)
    keys1 = peer_keys1[0].astype(BF16)
    keys2 = peer_keys2[0].astype(BF16)
    u_tab = peer_u[0].astype(BF16)
    v_tab_t = peer_v[0].T.astype(BF16)

    x_all, ret_p, conv_p = _mixer_prompt(x_prompt, n_sample_rows, gmix, w_in_b, conv_w[0], w_out_b)
    x_all, ret_s, conv_s = _mixer_sample(x_sample, x_all, state_ret[0], state_conv[0], PAST_LEN,
                                         gmix, w_in_b, conv_w[0], w_out_b)
    y_prompt, y_sample = _peer(x_all, n_prompt_rows, gffn, wq_t, keys1, keys2, u_tab, v_tab_t, gfin)
    return (y_prompt.reshape(batch, seq, d), y_sample.reshape(dec_batch, dec_seq, d),
            ret_p[None], conv_p[None], ret_s[None], conv_s[None])
```

```python
import functools

import numpy as np
import jax
import jax.numpy as jnp
from jax import lax
from jax.experimental import pallas as pl
from jax.experimental.pallas import tpu as pltpu

F32 = jnp.float32
BF16 = jnp.bfloat16

H_RET = 4
DK_RET = 128
D_RET = H_RET * DK_RET
D_CONV = 512
CONV_W = 3
CHUNK = 64
ROPE_THETA = 10000.0
RMS_EPS = 1e-6
PEER_HEADS = 8
N_KEYS = 128
PEER_TOPK = 16
D_KEY_HALF = 128
PAST_LEN = 4096

ROW_BLOCK = 128
LANES = 128
SUBLANES = 8
BF16_ROWS = 16
VMEM_LIMIT = 56 * 1024 * 1024
ROWS_PER_STEP = 16
UNIT_ROWS = 8


def _log_decay():
    return np.log1p(-np.exp2(-5.0 - np.arange(H_RET))).astype(np.float32).astype(np.float64)


def _retention_tables(chunk):
    lg = _log_decay()[:, None, None]
    idx = np.arange(ROW_BLOCK)
    ci, cj = idx[:, None] // chunk, idx[None, :] // chunk
    dist = (idx[:, None] - idx[None, :]).astype(np.float64)
    same = np.exp(lg * np.abs(dist)[None])
    later = np.exp(lg * dist[None])
    mask = np.where((ci == cj)[None], same, np.where((ci > cj)[None], later, 0.0))
    return mask.astype(np.float32)


def _prompt_tables():
    lg = _log_decay()
    idx = np.arange(ROW_BLOCK, dtype=np.float64)
    mask = _retention_tables(CHUNK)
    qd = np.exp(lg[:, None] * (idx[None] + 1.0))
    kd = np.exp(lg[:, None] * (ROW_BLOCK - 1.0 - idx[None]))
    cd = np.exp(lg * ROW_BLOCK)
    bc = lambda t: np.broadcast_to(t[:, :, None], (H_RET, ROW_BLOCK, DK_RET)).astype(np.float32)
    return mask, bc(qd), bc(kd), [float(c) for c in cd]


def _sample_tables(seq):
    lg = _log_decay()
    idx = np.arange(ROW_BLOCK)
    pos = (idx % seq).astype(np.float64)
    same_seq = (idx[:, None] // seq) == (idx[None, :] // seq)
    dist = np.abs(pos[:, None] - pos[None, :])
    mask = np.where(same_seq[None], np.exp(lg[:, None, None] * dist[None]), 0.0).astype(np.float32)
    qd = np.exp(lg[:, None] * (pos[None] + 1.0))
    kd = np.exp(lg[:, None] * (seq - 1.0 - pos[None]))
    cd = np.exp(lg * seq)
    bc = lambda t: np.broadcast_to(t[:, :, None], (H_RET, ROW_BLOCK, DK_RET)).astype(np.float32)
    return mask, bc(qd), bc(kd), [float(c) for c in cd]


def _rope_tables(pos):
    half = DK_RET // 2
    inv = ROPE_THETA ** (-jnp.arange(half, dtype=F32) / half)
    ang = pos.astype(F32)[:, None] * inv[None, :]
    cos, sin = jnp.cos(ang), jnp.sin(ang)
    return jnp.concatenate([cos, cos], axis=1), jnp.concatenate([-sin, sin], axis=1)


def _rms_rows(x):
    return x * lax.rsqrt(jnp.mean(x * x, axis=-1, keepdims=True) + RMS_EPS)


def _project_and_rotate(x, gmix_ref, w_in_ref, cos_ref, sin_ref, proj_ref):
    h = (_rms_rows(x) * gmix_ref[...]).astype(BF16)
    proj_ref[...] = jnp.dot(h, w_in_ref[...], preferred_element_type=F32)
    cs, sn = cos_ref[...], sin_ref[...]
    scale = DK_RET ** -0.5
    for hh in range(2 * H_RET):
        sl = slice(hh * DK_RET, (hh + 1) * DK_RET)
        t = proj_ref[:, sl]
        t = t * cs + pltpu.roll(t, DK_RET // 2, axis=1) * sn
        proj_ref[:, sl] = t * scale if hh < H_RET else t


def _head_cols(group, h):
    start = group * D_RET + h * DK_RET
    return slice(start, start + DK_RET)


def _gate_and_store(o, g, mix_ref, rows, h):
    o = _rms_rows(o)
    mix_ref[rows, _head_cols(0, h)] = (o * (g * (1.0 / (1.0 + jnp.exp(-g))))).astype(BF16)


def _short_conv(u, prev0, prev1, pos, w_ref):
    p1 = jnp.where(pos == 0, prev1, pltpu.roll(u, 1, axis=0))
    p2 = jnp.where(pos == 0, prev0, jnp.where(pos == 1, prev1, pltpu.roll(u, 2, axis=0)))
    return w_ref[0:1, :] * p2 + w_ref[1:2, :] * p1 + w_ref[2:3, :] * u


def _mixer_prompt_kernel(cd, x_ref, cos_ref, sin_ref, gmix_ref, w_in_ref, conv_w_ref, w_out_ref,
                         mask_ref, qd_ref, kd_ref,
                         out_ref, state_ref, conv_ref,
                         proj_ref, mix_ref):
    rows_total = x_ref.shape[0]

    @pl.when(pl.program_id(1) == 0)
    def _():
        state_ref[...] = jnp.zeros_like(state_ref)
        conv_ref[...] = jnp.zeros_like(conv_ref)

    x = x_ref[...]
    _project_and_rotate(x, gmix_ref, w_in_ref, cos_ref, sin_ref, proj_ref)

    for blk in range(rows_total // ROW_BLOCK):
        rows = slice(blk * ROW_BLOCK, (blk + 1) * ROW_BLOCK)
        for h in range(H_RET):
            q = proj_ref[rows, _head_cols(0, h)]
            k = proj_ref[rows, _head_cols(1, h)]
            vb = proj_ref[rows, _head_cols(2, h)].astype(BF16)
            a = lax.dot_general(q.astype(BF16), k.astype(BF16), (((1,), (1,)), ((), ())),
                                preferred_element_type=F32)
            s = state_ref[h]
            o = (jnp.dot((a * mask_ref[h]).astype(BF16), vb, preferred_element_type=F32)
                 + jnp.dot((q * qd_ref[h]).astype(BF16), s.astype(BF16), preferred_element_type=F32))
            kt = (k * kd_ref[h]).T.astype(BF16)
            state_ref[h] = cd[h] * s + jnp.dot(kt, vb, preferred_element_type=F32)
            _gate_and_store(o, proj_ref[rows, _head_cols(3, h)], mix_ref, rows, h)

    base = 4 * D_RET
    u = proj_ref[:, base + D_CONV:base + 2 * D_CONV] * proj_ref[:, base + 2 * D_CONV:base + 3 * D_CONV]
    pos = lax.broadcasted_iota(jnp.int32, u.shape, 0)
    y = _short_conv(u, conv_ref[0:1, :], conv_ref[1:2, :], pos, conv_w_ref)
    conv_ref[...] = u[rows_total - (CONV_W - 1):, :]
    mix_ref[:, D_RET:] = (proj_ref[:, base:base + D_CONV] * y).astype(BF16)

    out_ref[...] = x + jnp.dot(mix_ref[...], w_out_ref[...], preferred_element_type=F32)


def _mixer_sample_kernel(cd, seq, x_ref, cos_ref, sin_ref, gmix_ref, w_in_ref, conv_w_ref, w_out_ref,
                         mask_ref, qd_ref, kd_ref, state_in_ref, conv_in_ref,
                         out_ref, state_ref, conv_ref,
                         proj_ref, mix_ref):
    nseq = ROW_BLOCK // seq
    x = x_ref[...]
    _project_and_rotate(x, gmix_ref, w_in_ref, cos_ref, sin_ref, proj_ref)

    rows = slice(0, ROW_BLOCK)
    row_id = lax.broadcasted_iota(jnp.int32, (ROW_BLOCK, DK_RET), 0)
    for h in range(H_RET):
        q = proj_ref[rows, _head_cols(0, h)]
        k = proj_ref[rows, _head_cols(1, h)]
        vb = proj_ref[rows, _head_cols(2, h)].astype(BF16)
        a = lax.dot_general(q.astype(BF16), k.astype(BF16), (((1,), (1,)), ((), ())),
                            preferred_element_type=F32)
        qs = (q * qd_ref[h]).astype(BF16)
        ks = k * kd_ref[h]
        inter = []
        for b in range(nseq):
            s = state_in_ref[b, h]
            inter.append(jnp.dot(qs[b * seq:(b + 1) * seq, :], s.astype(BF16), preferred_element_type=F32))
            in_seq = (row_id >= b * seq) & (row_id < (b + 1) * seq)
            kt = jnp.where(in_seq, ks, 0.0).T.astype(BF16)
            state_ref[b, h] = cd[h] * s + jnp.dot(kt, vb, preferred_element_type=F32)
        o = (jnp.dot((a * mask_ref[h]).astype(BF16), vb, preferred_element_type=F32)
             + jnp.concatenate(inter, axis=0))
        _gate_and_store(o, proj_ref[rows, _head_cols(3, h)], mix_ref, rows, h)

    base = 4 * D_RET
    u = proj_ref[:, base + D_CONV:base + 2 * D_CONV] * proj_ref[:, base + 2 * D_CONV:base + 3 * D_CONV]
    pos = lax.broadcasted_iota(jnp.int32, u.shape, 0) % seq
    prev0 = jnp.concatenate([jnp.broadcast_to(conv_in_ref[b, 0:1, :], (seq, D_CONV)) for b in range(nseq)], axis=0)
    prev1 = jnp.concatenate([jnp.broadcast_to(conv_in_ref[b, 1:2, :], (seq, D_CONV)) for b in range(nseq)], axis=0)
    y = _short_conv(u, prev0, prev1, pos, conv_w_ref)
    for b in range(nseq):
        conv_ref[b] = u[(b + 1) * seq - (CONV_W - 1):(b + 1) * seq, :]
    mix_ref[:, D_RET:] = (proj_ref[:, base:base + D_CONV] * y).astype(BF16)

    out_ref[...] = x + jnp.dot(mix_ref[...], w_out_ref[...], preferred_element_type=F32)


def _full(shape):
    return pl.BlockSpec(shape, lambda *_: (0,) * len(shape))


def _mixer_prompt(x, gmix, w_in, conv_w, w_out):
    batch, seq, d = x.shape
    d_in = w_in.shape[1]
    rows = 512 if seq % 512 == 0 else ROW_BLOCK
    assert seq % rows == 0 and seq > CHUNK and ROW_BLOCK % CHUNK == 0
    tiles = seq // rows
    mask, qd, kd, cd = _prompt_tables()
    cos, sin = _rope_tables(jnp.arange(seq, dtype=jnp.int32))
    hshape = (H_RET, ROW_BLOCK, DK_RET)
    return pl.pallas_call(
        functools.partial(_mixer_prompt_kernel, cd),
        grid=(batch, tiles),
        in_specs=[
            pl.BlockSpec((None, rows, d), lambda b, l: (b, l, 0)),
            pl.BlockSpec((rows, DK_RET), lambda b, l: (l, 0)),
            pl.BlockSpec((rows, DK_RET), lambda b, l: (l, 0)),
            _full((1, d)), _full((d, d_in)), _full((CONV_W, D_CONV)), _full((d, d)),
            _full(hshape), _full(hshape), _full(hshape),
        ],
        out_specs=[
            pl.BlockSpec((rows, d), lambda b, l: (b * tiles + l, 0)),
            pl.BlockSpec((None, H_RET, DK_RET, DK_RET), lambda b, l: (b, 0, 0, 0)),
            pl.BlockSpec((None, CONV_W - 1, D_CONV), lambda b, l: (b, 0, 0)),
        ],
        out_shape=[
            jax.ShapeDtypeStruct((batch * seq, d), F32),
            jax.ShapeDtypeStruct((batch, H_RET, DK_RET, DK_RET), F32),
            jax.ShapeDtypeStruct((batch, CONV_W - 1, D_CONV), F32),
        ],
        scratch_shapes=[pltpu.VMEM((rows, d_in), F32), pltpu.VMEM((rows, d), BF16)],
        compiler_params=pltpu.CompilerParams(
            dimension_semantics=("parallel", "arbitrary"), vmem_limit_bytes=VMEM_LIMIT),
        name="mixer_prompt",
    )(x, cos, sin, gmix, w_in, conv_w, w_out, jnp.asarray(mask), jnp.asarray(qd), jnp.asarray(kd))


def _mixer_sample(x, state, conv_state, past_len, gmix, w_in, conv_w, w_out):
    batch, seq, d = x.shape
    d_in = w_in.shape[1]
    assert seq <= CHUNK and ROW_BLOCK % seq == 0 and seq >= CONV_W - 1
    nseq = ROW_BLOCK // seq
    assert batch % nseq == 0
    mask, qd, kd, cd = _sample_tables(seq)
    cos, sin = _rope_tables(past_len + jnp.tile(jnp.arange(seq, dtype=jnp.int32), nseq))
    hshape = (H_RET, ROW_BLOCK, DK_RET)
    x2 = x.reshape(batch * seq, d)
    return pl.pallas_call(
        functools.partial(_mixer_sample_kernel, cd, seq),
        grid=(batch // nseq,),
        in_specs=[
            pl.BlockSpec((ROW_BLOCK, d), lambda g: (g, 0)),
            _full((ROW_BLOCK, DK_RET)), _full((ROW_BLOCK, DK_RET)),
            _full((1, d)), _full((d, d_in)), _full((CONV_W, D_CONV)), _full((d, d)),
            _full(hshape), _full(hshape), _full(hshape),
            pl.BlockSpec((nseq, H_RET, DK_RET, DK_RET), lambda g: (g, 0, 0, 0)),
            pl.BlockSpec((nseq, CONV_W - 1, D_CONV), lambda g: (g, 0, 0)),
        ],
        out_specs=[
            pl.BlockSpec((ROW_BLOCK, d), lambda g: (g, 0)),
            pl.BlockSpec((nseq, H_RET, DK_RET, DK_RET), lambda g: (g, 0, 0, 0)),
            pl.BlockSpec((nseq, CONV_W - 1, D_CONV), lambda g: (g, 0, 0)),
        ],
        out_shape=[
            jax.ShapeDtypeStruct((batch * seq, d), F32),
            jax.ShapeDtypeStruct((batch, H_RET, DK_RET, DK_RET), F32),
            jax.ShapeDtypeStruct((batch, CONV_W - 1, D_CONV), F32),
        ],
        scratch_shapes=[pltpu.VMEM((ROW_BLOCK, d_in), F32), pltpu.VMEM((ROW_BLOCK, d), BF16)],
        compiler_params=pltpu.CompilerParams(
            dimension_semantics=("parallel",), vmem_limit_bytes=VMEM_LIMIT),
        name="mixer_sample",
    )(x2, cos, sin, gmix, w_in, conv_w, w_out, jnp.asarray(mask), jnp.asarray(qd), jnp.asarray(kd),
      state, conv_state)


def _sort16_pairs():
    def merge(lo, hi, r):
        step = r * 2
        if step < hi - lo:
            yield from merge(lo, hi, step)
            yield from merge(lo + r, hi, step)
            for i in range(lo + r, hi - r, step):
                yield (i, i + r)
        else:
            yield (lo, lo + r)

    def sort(lo, hi):
        if hi - lo >= 1:
            mid = lo + (hi - lo) // 2
            yield from sort(lo, mid)
            yield from sort(mid + 1, hi)
            yield from merge(lo, hi, 1)

    return tuple(sort(0, PEER_TOPK - 1))


_SORT16 = _sort16_pairs()


def _exchange(vals, i, j):
    a, b = vals[i], vals[j]
    if b is None:
        return
    if a is None:
        vals[i], vals[j] = b, None
        return
    vals[i], vals[j] = jnp.maximum(a, b), jnp.minimum(a, b)


def _sort16_desc(vals):
    vals = list(vals)
    for i, j in _SORT16:
        _exchange(vals, i, j)
    return vals


def _merge_top16(a, b):
    vals = []
    for k in range(PEER_TOPK):
        x, y = a[k], b[PEER_TOPK - 1 - k]
        vals.append(y if x is None else (x if y is None else jnp.maximum(x, y)))
    stride = PEER_TOPK // 2
    while stride:
        for i in range(PEER_TOPK):
            if i & stride == 0:
                _exchange(vals, i, i + stride)
        stride //= 2
    return vals


def _top16_of_rows(s):
    vals = _sort16_desc([s[r * SUBLANES:(r + 1) * SUBLANES, :] for r in range(N_KEYS // SUBLANES)])
    shift = SUBLANES // 2
    while shift:
        vals = _merge_top16(vals, [pltpu.roll(v, shift, axis=0) for v in vals])
        shift //= 2
    return vals


_REST_PAIRS = tuple((k1, k2) for k1 in range(2, PEER_TOPK) for k2 in range(1, PEER_TOPK)
                    if (k1 + 1) * (k2 + 1) <= PEER_TOPK)
assert len(_REST_PAIRS) <= PEER_TOPK


def _top16_sums(v1, v2):
    pad = lambda lst: lst + [None] * (PEER_TOPK - len(lst))
    row0 = [v1[0] + v2[k] for k in range(PEER_TOPK)]
    col0 = pad([v1[k] + v2[0] for k in range(1, PEER_TOPK)])
    row1 = pad([v1[1] + v2[k] for k in range(1, PEER_TOPK) if 2 * (k + 1) <= PEER_TOPK])
    rest = _sort16_desc(pad([v1[a] + v2[b] for a, b in _REST_PAIRS]))
    return _merge_top16(_merge_top16(row0, col0), _merge_top16(row1, rest))


def _count_leading(vals, pred):
    g8 = pred(vals[7])
    g4 = pred(jnp.where(g8, vals[11], vals[3]))
    g2 = pred(jnp.where(g8, jnp.where(g4, vals[13], vals[9]), jnp.where(g4, vals[5], vals[1])))
    quads = [jnp.where(g2, vals[4 * m + 2], vals[4 * m]) for m in range(4)]
    g1 = pred(jnp.where(g8, jnp.where(g4, quads[3], quads[2]), jnp.where(g4, quads[1], quads[0])))
    g16 = pred(vals[15])
    cnt = (jnp.where(g8, jnp.where(g4, 12.0, 8.0), jnp.where(g4, 4.0, 0.0))
           + jnp.where(g2, jnp.where(g1, 3.0, 2.0), jnp.where(g1, 1.0, 0.0)))
    return jnp.where(g16, 16.0, cnt)


def _pair_words(x):
    bits = pltpu.bitcast(x.astype(BF16).astype(F32), jnp.uint32)
    return bits | (bits >> 16)


def _route_lane_block(s1_ref, s2_ref, a_ref, c_ref, b_ref, r_ref, h, lanes):
    s1 = s1_ref[:, lanes]
    s2 = s2_ref[:, lanes]
    v1 = _top16_of_rows(s1)
    v2 = _top16_of_rows(s2)
    top = _top16_sums(v1, v2)
    tau = top[PEER_TOPK - 1]
    z = jnp.ones_like(tau)
    for k in range(1, PEER_TOPK):
        z = z + jnp.exp(top[k] - top[0])
    half_inv_z = 0.5 / z
    for r in range(N_KEYS // SUBLANES):
        rs = slice(r * SUBLANES, (r + 1) * SUBLANES)
        x1, x2 = s1[rs, :], s2[rs, :]
        cnt = _count_leading(v2, lambda v: x1 + v >= tau)
        rank = _count_leading(v2, lambda v: v > x2)
        a_ref[h, rs, lanes] = _pair_words(jnp.exp(x1 - v1[0]) * half_inv_z)
        c_ref[h, rs, lanes] = _pair_words(cnt)
        b_ref[rs, lanes] = jnp.exp(x2 - v2[0])
        r_ref[rs, lanes] = rank


_GELU_C0 = float(np.sqrt(2.0 / np.pi))
_GELU_C1 = float(np.sqrt(2.0 / np.pi) * 0.044715)


def _twice_gelu_tanh(x):
    return x * (1.0 + jnp.tanh(x * (_GELU_C0 + _GELU_C1 * (x * x))))


def _peer_kernel(rows_per_step, n_prompt_tiles, xp_ref, xs_ref, gffn_ref, wq_ref, k1_ref, k2_ref, u_ref, vt_ref,
                 gfin_ref, yp_ref, ys_ref,
                 ht_ref, acc_ref, s1_ref, s2_ref, a_ref, c_ref, b_ref, r_ref, bb_ref, rb_ref,
                 act_ref, wact_ref):
    t = pl.program_id(0)
    e = pl.program_id(1)
    tokens = xp_ref.shape[0]
    n_lane_blocks = tokens // LANES
    groups = N_KEYS // BF16_ROWS

    def tile_input():
        return jnp.where(t < n_prompt_tiles, xp_ref[...], xs_ref[...])

    @pl.when(e == 0)
    def _route():
        h2 = _rms_rows(tile_input()) * gffn_ref[...]
        ht_ref[...] = h2.T.astype(BF16)
        acc_ref[...] = jnp.zeros_like(acc_ref)

        def head_body(h, carry):
            q0 = pl.multiple_of(h * 2 * D_KEY_HALF, 2 * D_KEY_HALF)
            qt = jnp.dot(wq_ref[pl.ds(q0, 2 * D_KEY_HALF), :], ht_ref[...], preferred_element_type=F32)
            s1_ref[...] = jnp.dot(k1_ref[h], qt[:D_KEY_HALF].astype(BF16), preferred_element_type=F32)
            s2_ref[...] = jnp.dot(k2_ref[h], qt[D_KEY_HALF:].astype(BF16), preferred_element_type=F32)

            def lane_body(lb, c2):
                lanes = pl.ds(pl.multiple_of(lb * LANES, LANES), LANES)
                _route_lane_block(s1_ref, s2_ref, a_ref, c_ref, b_ref, r_ref, h, lanes)
                return c2

            lax.fori_loop(0, n_lane_blocks, lane_body, 0)
            bb_ref[h] = b_ref[...].astype(BF16).reshape(groups, BF16_ROWS, tokens)
            rb_ref[h] = r_ref[...].astype(BF16).reshape(groups, BF16_ROWS, tokens)
            return carry

        lax.fori_loop(0, PEER_HEADS, head_body, 0)

    n_split = 2 if tokens % (2 * 2 * LANES) == 0 else 1
    width = tokens // n_split
    rows_half = UNIT_ROWS
    units = [(eh, sp) for eh in range(rows_per_step // UNIT_ROWS) for sp in range(n_split)]
    row0 = pl.multiple_of(e * rows_per_step, SUBLANES)
    zero = jnp.zeros((), BF16)

    def row_as_bf16(words_ref, h, ii, lanes):
        tile = words_ref[h, pl.ds(row0 + ii // SUBLANES * SUBLANES, SUBLANES), lanes]
        sub = ii % SUBLANES
        return pltpu.bitcast(jnp.broadcast_to(tile[sub:sub + 1, :], (SUBLANES, LANES)), BF16)

    def gate_rows(ii, lanes):
        w = None
        for h in range(PEER_HEADS):
            c16 = row_as_bf16(c_ref, h, ii, lanes)
            a16 = row_as_bf16(a_ref, h, ii, lanes)
            term = jnp.where(rb_ref[h, :, :, lanes] < c16[None], bb_ref[h, :, :, lanes], zero) * a16[None]
            w = term if w is None else w + term
        rs = slice(ii * N_KEYS, (ii + 1) * N_KEYS)
        act = _twice_gelu_tanh(act_ref[rs, lanes]).reshape(groups, BF16_ROWS, LANES)
        wact_ref[rs, lanes] = (w * act).reshape(N_KEYS, LANES)

    for eh, sp in units:
        cols = slice(sp * width, (sp + 1) * width)
        rs = slice(eh * rows_half * N_KEYS, (eh + 1) * rows_half * N_KEYS)
        act_ref[rs, cols] = jnp.dot(u_ref[rs, :], ht_ref[:, cols], preferred_element_type=F32).astype(BF16)
    for eh, sp in units:
        for lb in range(width // LANES):
            lane0 = sp * width + lb * LANES
            for ii in range(eh * rows_half, (eh + 1) * rows_half):
                gate_rows(ii, slice(lane0, lane0 + LANES))
    for eh, sp in units:
        cols = slice(sp * width, (sp + 1) * width)
        rs = slice(eh * rows_half * N_KEYS, (eh + 1) * rows_half * N_KEYS)
        acc_ref[:, cols] += jnp.dot(vt_ref[:, rs], wact_ref[rs, cols], preferred_element_type=F32)

    @pl.when(e == pl.num_programs(1) - 1)
    def _():
        out = tile_input() + acc_ref[...].T
        y = _rms_rows(out) * gfin_ref[...]

        @pl.when(t < n_prompt_tiles)
        def _():
            yp_ref[...] = y

        @pl.when(t >= n_prompt_tiles)
        def _():
            ys_ref[...] = y


def _peer(x_prompt, x_sample, gffn, wq_t, keys1, keys2, u_tab, v_tab_t, gfin):
    n_prompt_rows, d = x_prompt.shape
    n_sample_rows = x_sample.shape[0]
    total = n_prompt_rows + n_sample_rows
    n_experts = u_tab.shape[0]
    tokens = next(t for t in (512, 256, 128) if n_prompt_rows % t == 0 and n_sample_rows % t == 0)
    n_prompt_tiles = n_prompt_rows // tokens
    prompt_block = lambda t, e: (jnp.minimum(t, n_prompt_tiles - 1), 0)
    sample_block = lambda t, e: (jnp.maximum(t - n_prompt_tiles, 0), 0)
    rows_per_step = ROWS_PER_STEP
    experts_per_step = rows_per_step * N_KEYS
    groups = N_KEYS // BF16_ROWS
    head_shape = (PEER_HEADS, N_KEYS, tokens)
    return pl.pallas_call(
        functools.partial(_peer_kernel, rows_per_step, n_prompt_tiles),
        grid=(total // tokens, n_experts // experts_per_step),
        in_specs=[
            pl.BlockSpec((tokens, d), prompt_block),
            pl.BlockSpec((tokens, d), sample_block),
            _full((1, d)),
            _full(wq_t.shape), _full(keys1.shape), _full(keys2.shape),
            pl.BlockSpec((experts_per_step, d), lambda t, e: (e, 0)),
            pl.BlockSpec((d, experts_per_step), lambda t, e: (0, e)),
            _full((1, d)),
        ],
        out_specs=[pl.BlockSpec((tokens, d), prompt_block), pl.BlockSpec((tokens, d), sample_block)],
        out_shape=[
            jax.ShapeDtypeStruct((n_prompt_rows, d), F32),
            jax.ShapeDtypeStruct((n_sample_rows, d), F32),
        ],
        scratch_shapes=[
            pltpu.VMEM((d, tokens), BF16),
            pltpu.VMEM((d, tokens), F32),
            pltpu.VMEM((N_KEYS, tokens), F32), pltpu.VMEM((N_KEYS, tokens), F32),
            pltpu.VMEM(head_shape, jnp.uint32), pltpu.VMEM(head_shape, jnp.uint32),
            pltpu.VMEM((N_KEYS, tokens), F32), pltpu.VMEM((N_KEYS, tokens), F32),
            pltpu.VMEM((PEER_HEADS, groups, BF16_ROWS, tokens), BF16),
            pltpu.VMEM((PEER_HEADS, groups, BF16_ROWS, tokens), BF16),
            pltpu.VMEM((experts_per_step, tokens), BF16),
            pltpu.VMEM((experts_per_step, tokens), BF16),
        ],
        compiler_params=pltpu.CompilerParams(
            dimension_semantics=("arbitrary", "arbitrary"), vmem_limit_bytes=VMEM_LIMIT),
        name="peer",
    )(x_prompt, x_sample, gffn, wq_t, keys1, keys2, u_tab, v_tab_t, gfin)


def kernel(x_prompt, x_sample, state_ret, state_conv, norm_mix_g, w_in, conv_w, w_out, norm_ffn_g,
           peer_wq, peer_keys1, peer_keys2, peer_u, peer_v, norm_final_g):
    depth = w_in.shape[0]
    assert depth == 1
    batch, seq, d = x_prompt.shape
    dec_batch, dec_seq, _ = x_sample.shape

    gmix = norm_mix_g[0][None, :]
    gffn = norm_ffn_g[0][None, :]
    gfin = norm_final_g[None, :]
    w_in_b = w_in[0].astype(BF16)
    w_out_b = w_out[0].astype(BF16)
    wq_t = peer_wq[0].T.astype(BF16)
    keys1 = peer_keys1[0].astype(BF16)
    keys2 = peer_keys2[0].astype(BF16)
    u_tab = peer_u[0].astype(BF16)
    v_tab_t = peer_v[0].T.astype(BF16)

    xp, ret_p, conv_p = _mixer_prompt(x_prompt, gmix, w_in_b, conv_w[0], w_out_b)
    xs, ret_s, conv_s = _mixer_sample(x_sample, state_ret[0], state_conv[0], PAST_LEN,
                                      gmix, w_in_b, conv_w[0], w_out_b)
    y_prompt, y_sample = _peer(xp, xs, gffn, wq_t, keys1, keys2, u_tab, v_tab_t, gfin)
    return (y_prompt.reshape(batch, seq, d), y_sample.reshape(dec_batch, dec_seq, d),
            ret_p[None], conv_p[None], ret_s[None], conv_s[None])
```

```python
import functools

import numpy as np
import jax
import jax.numpy as jnp
from jax import lax
from jax.experimental import pallas as pl
from jax.experimental.pallas import tpu as pltpu

F32 = jnp.float32
BF16 = jnp.bfloat16

H_RET = 4
DK_RET = 128
D_RET = H_RET * DK_RET
D_CONV = 512
CONV_W = 3
CHUNK = 64
ROPE_THETA = 10000.0
RMS_EPS = 1e-6
PEER_HEADS = 8
N_KEYS = 128
PEER_TOPK = 16
D_KEY_HALF = 128
PAST_LEN = 4096

ROW_BLOCK = 128
LANES = 128
SUBLANES = 8
BF16_ROWS = 16
VMEM_LIMIT = 56 * 1024 * 1024
ROWS_PER_STEP = 16
UNIT_ROWS = 8


def _log_decay():
    return np.log1p(-np.exp2(-5.0 - np.arange(H_RET))).astype(np.float32).astype(np.float64)


def _retention_tables(chunk):
    lg = _log_decay()[:, None, None]
    idx = np.arange(ROW_BLOCK)
    ci, cj = idx[:, None] // chunk, idx[None, :] // chunk
    dist = (idx[:, None] - idx[None, :]).astype(np.float64)
    same = np.exp(lg * np.abs(dist)[None])
    later = np.exp(lg * dist[None])
    mask = np.where((ci == cj)[None], same, np.where((ci > cj)[None], later, 0.0))
    return mask.astype(np.float32)


def _prompt_tables():
    lg = _log_decay()
    idx = np.arange(ROW_BLOCK, dtype=np.float64)
    mask = _retention_tables(CHUNK)
    qd = np.exp(lg[:, None] * (idx[None] + 1.0))
    kd = np.exp(lg[:, None] * (ROW_BLOCK - 1.0 - idx[None]))
    cd = np.exp(lg * ROW_BLOCK)
    bc = lambda t: np.broadcast_to(t[:, :, None], (H_RET, ROW_BLOCK, DK_RET)).astype(np.float32)
    return mask, bc(qd), bc(kd), [float(c) for c in cd]


def _sample_tables(seq):
    lg = _log_decay()
    idx = np.arange(ROW_BLOCK)
    pos = (idx % seq).astype(np.float64)
    same_seq = (idx[:, None] // seq) == (idx[None, :] // seq)
    dist = np.abs(pos[:, None] - pos[None, :])
    mask = np.where(same_seq[None], np.exp(lg[:, None, None] * dist[None]), 0.0).astype(np.float32)
    qd = np.exp(lg[:, None] * (pos[None] + 1.0))
    kd = np.exp(lg[:, None] * (seq - 1.0 - pos[None]))
    cd = np.exp(lg * seq)
    bc = lambda t: np.broadcast_to(t[:, :, None], (H_RET, ROW_BLOCK, DK_RET)).astype(np.float32)
    return mask, bc(qd), bc(kd), [float(c) for c in cd]


def _rope_tables(pos):
    half = DK_RET // 2
    inv = ROPE_THETA ** (-jnp.arange(half, dtype=F32) / half)
    ang = pos.astype(F32)[:, None] * inv[None, :]
    cos, sin = jnp.cos(ang), jnp.sin(ang)
    return jnp.concatenate([cos, cos], axis=1), jnp.concatenate([-sin, sin], axis=1)


def _rms_rows(x):
    return x * lax.rsqrt(jnp.mean(x * x, axis=-1, keepdims=True) + RMS_EPS)


def _project_and_rotate(x, gmix_ref, w_in_ref, cos_ref, sin_ref, proj_ref):
    h = (_rms_rows(x) * gmix_ref[...]).astype(BF16)
    proj_ref[...] = jnp.dot(h, w_in_ref[...], preferred_element_type=F32)
    cs, sn = cos_ref[...], sin_ref[...]
    scale = DK_RET ** -0.5
    for hh in range(2 * H_RET):
        sl = slice(hh * DK_RET, (hh + 1) * DK_RET)
        t = proj_ref[:, sl]
        t = t * cs + pltpu.roll(t, DK_RET // 2, axis=1) * sn
        proj_ref[:, sl] = t * scale if hh < H_RET else t


def _head_cols(group, h):
    start = group * D_RET + h * DK_RET
    return slice(start, start + DK_RET)


def _gate_and_store(o, g, mix_ref, rows, h):
    o = _rms_rows(o)
    mix_ref[rows, _head_cols(0, h)] = (o * (g * (1.0 / (1.0 + jnp.exp(-g))))).astype(BF16)


def _short_conv(u, prev0, prev1, pos, w_ref):
    p1 = jnp.where(pos == 0, prev1, pltpu.roll(u, 1, axis=0))
    p2 = jnp.where(pos == 0, prev0, jnp.where(pos == 1, prev1, pltpu.roll(u, 2, axis=0)))
    return w_ref[0:1, :] * p2 + w_ref[1:2, :] * p1 + w_ref[2:3, :] * u


def _mixer_prompt_kernel(cd, x_ref, cos_ref, sin_ref, gmix_ref, w_in_ref, conv_w_ref, w_out_ref,
                         mask_ref, qd_ref, kd_ref,
                         out_ref, state_ref, conv_ref,
                         proj_ref, mix_ref):
    rows_total = x_ref.shape[0]

    @pl.when(pl.program_id(1) == 0)
    def _():
        state_ref[...] = jnp.zeros_like(state_ref)
        conv_ref[...] = jnp.zeros_like(conv_ref)

    x = x_ref[...]
    _project_and_rotate(x, gmix_ref, w_in_ref, cos_ref, sin_ref, proj_ref)

    for blk in range(rows_total // ROW_BLOCK):
        rows = slice(blk * ROW_BLOCK, (blk + 1) * ROW_BLOCK)
        for h in range(H_RET):
            q = proj_ref[rows, _head_cols(0, h)]
            k = proj_ref[rows, _head_cols(1, h)]
            vb = proj_ref[rows, _head_cols(2, h)].astype(BF16)
            a = lax.dot_general(q.astype(BF16), k.astype(BF16), (((1,), (1,)), ((), ())),
                                preferred_element_type=F32)
            s = state_ref[h]
            o = (jnp.dot((a * mask_ref[h]).astype(BF16), vb, preferred_element_type=F32)
                 + jnp.dot((q * qd_ref[h]).astype(BF16), s.astype(BF16), preferred_element_type=F32))
            kt = (k * kd_ref[h]).T.astype(BF16)
            state_ref[h] = cd[h] * s + jnp.dot(kt, vb, preferred_element_type=F32)
            _gate_and_store(o, proj_ref[rows, _head_cols(3, h)], mix_ref, rows, h)

    base = 4 * D_RET
    u = proj_ref[:, base + D_CONV:base + 2 * D_CONV] * proj_ref[:, base + 2 * D_CONV:base + 3 * D_CONV]
    pos = lax.broadcasted_iota(jnp.int32, u.shape, 0)
    y = _short_conv(u, conv_ref[0:1, :], conv_ref[1:2, :], pos, conv_w_ref)
    conv_ref[...] = u[rows_total - (CONV_W - 1):, :]
    mix_ref[:, D_RET:] = (proj_ref[:, base:base + D_CONV] * y).astype(BF16)

    out_ref[...] = x + jnp.dot(mix_ref[...], w_out_ref[...], preferred_element_type=F32)


def _mixer_sample_kernel(cd, seq, x_ref, cos_ref, sin_ref, gmix_ref, w_in_ref, conv_w_ref, w_out_ref,
                         mask_ref, qd_ref, kd_ref, state_in_ref, conv_in_ref,
                         out_ref, state_ref, conv_ref,
                         proj_ref, mix_ref):
    nseq = ROW_BLOCK // seq
    x = x_ref[...]
    _project_and_rotate(x, gmix_ref, w_in_ref, cos_ref, sin_ref, proj_ref)

    rows = slice(0, ROW_BLOCK)
    row_id = lax.broadcasted_iota(jnp.int32, (ROW_BLOCK, DK_RET), 0)
    for h in range(H_RET):
        q = proj_ref[rows, _head_cols(0, h)]
        k = proj_ref[rows, _head_cols(1, h)]
        vb = proj_ref[rows, _head_cols(2, h)].astype(BF16)
        a = lax.dot_general(q.astype(BF16), k.astype(BF16), (((1,), (1,)), ((), ())),
                            preferred_element_type=F32)
        qs = (q * qd_ref[h]).astype(BF16)
        ks = k * kd_ref[h]
        inter = []
        for b in range(nseq):
            s = state_in_ref[b, h]
            inter.append(jnp.dot(qs[b * seq:(b + 1) * seq, :], s.astype(BF16), preferred_element_type=F32))
            in_seq = (row_id >= b * seq) & (row_id < (b + 1) * seq)
            kt = jnp.where(in_seq, ks, 0.0).T.astype(BF16)
            state_ref[b, h] = cd[h] * s + jnp.dot(kt, vb, preferred_element_type=F32)
        o = (jnp.dot((a * mask_ref[h]).astype(BF16), vb, preferred_element_type=F32)
             + jnp.concatenate(inter, axis=0))
        _gate_and_store(o, proj_ref[rows, _head_cols(3, h)], mix_ref, rows, h)

    base = 4 * D_RET
    u = proj_ref[:, base + D_CONV:base + 2 * D_CONV] * proj_ref[:, base + 2 * D_CONV:base + 3 * D_CONV]
    pos = lax.broadcasted_iota(jnp.int32, u.shape, 0) % seq
    prev0 = jnp.concatenate([jnp.broadcast_to(conv_in_ref[b, 0:1, :], (seq, D_CONV)) for b in range(nseq)], axis=0)
    prev1 = jnp.concatenate([jnp.broadcast_to(conv_in_ref[b, 1:2, :], (seq, D_CONV)) for b in range(nseq)], axis=0)
    y = _short_conv(u, prev0, prev1, pos, conv_w_ref)
    for b in range(nseq):
        conv_ref[b] = u[(b + 1) * seq - (CONV_W - 1):(b + 1) * seq, :]
    mix_ref[:, D_RET:] = (proj_ref[:, base:base + D_CONV] * y).astype(BF16)

    out_ref[...] = x + jnp.dot(mix_ref[...], w_out_ref[...], preferred_element_type=F32)


def _full(shape):
    return pl.BlockSpec(shape, lambda *_: (0,) * len(shape))


def _mixer_prompt(x, gmix, w_in, conv_w, w_out):
    batch, seq, d = x.shape
    d_in = w_in.shape[1]
    rows = 512 if seq % 512 == 0 else ROW_BLOCK
    assert seq % rows == 0 and seq > CHUNK and ROW_BLOCK % CHUNK == 0
    tiles = seq // rows
    mask, qd, kd, cd = _prompt_tables()
    cos, sin = _rope_tables(jnp.arange(seq, dtype=jnp.int32))
    hshape = (H_RET, ROW_BLOCK, DK_RET)
    return pl.pallas_call(
        functools.partial(_mixer_prompt_kernel, cd),
        grid=(batch, tiles),
        in_specs=[
            pl.BlockSpec((None, rows, d), lambda b, l: (b, l, 0)),
            pl.BlockSpec((rows, DK_RET), lambda b, l: (l, 0)),
            pl.BlockSpec((rows, DK_RET), lambda b, l: (l, 0)),
            _full((1, d)), _full((d, d_in)), _full((CONV_W, D_CONV)), _full((d, d)),
            _full(hshape), _full(hshape), _full(hshape),
        ],
        out_specs=[
            pl.BlockSpec((rows, d), lambda b, l: (b * tiles + l, 0)),
            pl.BlockSpec((None, H_RET, DK_RET, DK_RET), lambda b, l: (b, 0, 0, 0)),
            pl.BlockSpec((None, CONV_W - 1, D_CONV), lambda b, l: (b, 0, 0)),
        ],
        out_shape=[
            jax.ShapeDtypeStruct((batch * seq, d), F32),
            jax.ShapeDtypeStruct((batch, H_RET, DK_RET, DK_RET), F32),
            jax.ShapeDtypeStruct((batch, CONV_W - 1, D_CONV), F32),
        ],
        scratch_shapes=[pltpu.VMEM((rows, d_in), F32), pltpu.VMEM((rows, d), BF16)],
        compiler_params=pltpu.CompilerParams(
            dimension_semantics=("parallel", "arbitrary"), vmem_limit_bytes=VMEM_LIMIT),
        name="mixer_prompt",
    )(x, cos, sin, gmix, w_in, conv_w, w_out, jnp.asarray(mask), jnp.asarray(qd), jnp.asarray(kd))


def _mixer_sample(x, state, conv_state, past_len, gmix, w_in, conv_w, w_out):
    batch, seq, d = x.shape
    d_in = w_in.shape[1]
    assert seq <= CHUNK and ROW_BLOCK % seq == 0 and seq >= CONV_W - 1
    nseq = ROW_BLOCK // seq
    assert batch % nseq == 0
    mask, qd, kd, cd = _sample_tables(seq)
    cos, sin = _rope_tables(past_len + jnp.tile(jnp.arange(seq, dtype=jnp.int32), nseq))
    hshape = (H_RET, ROW_BLOCK, DK_RET)
    x2 = x.reshape(batch * seq, d)
    return pl.pallas_call(
        functools.partial(_mixer_sample_kernel, cd, seq),
        grid=(batch // nseq,),
        in_specs=[
            pl.BlockSpec((ROW_BLOCK, d), lambda g: (g, 0)),
            _full((ROW_BLOCK, DK_RET)), _full((ROW_BLOCK, DK_RET)),
            _full((1, d)), _full((d, d_in)), _full((CONV_W, D_CONV)), _full((d, d)),
            _full(hshape), _full(hshape), _full(hshape),
            pl.BlockSpec((nseq, H_RET, DK_RET, DK_RET), lambda g: (g, 0, 0, 0)),
            pl.BlockSpec((nseq, CONV_W - 1, D_CONV), lambda g: (g, 0, 0)),
        ],
        out_specs=[
            pl.BlockSpec((ROW_BLOCK, d), lambda g: (g, 0)),
            pl.BlockSpec((nseq, H_RET, DK_RET, DK_RET), lambda g: (g, 0, 0, 0)),
            pl.BlockSpec((nseq, CONV_W - 1, D_CONV), lambda g: (g, 0, 0)),
        ],
        out_shape=[
            jax.ShapeDtypeStruct((batch * seq, d), F32),
            jax.ShapeDtypeStruct((batch, H_RET, DK_RET, DK_RET), F32),
            jax.ShapeDtypeStruct((batch, CONV_W - 1, D_CONV), F32),
        ],
        scratch_shapes=[pltpu.VMEM((ROW_BLOCK, d_in), F32), pltpu.VMEM((ROW_BLOCK, d), BF16)],
        compiler_params=pltpu.CompilerParams(
            dimension_semantics=("parallel",), vmem_limit_bytes=VMEM_LIMIT),
        name="mixer_sample",
    )(x2, cos, sin, gmix, w_in, conv_w, w_out, jnp.asarray(mask), jnp.asarray(qd), jnp.asarray(kd),
      state, conv_state)


def _sort16_pairs():
    def merge(lo, hi, r):
        step = r * 2
        if step < hi - lo:
            yield from merge(lo, hi, step)
            yield from merge(lo + r, hi, step)
            for i in range(lo + r, hi - r, step):
                yield (i, i + r)
        else:
            yield (lo, lo + r)

    def sort(lo, hi):
        if hi - lo >= 1:
            mid = lo + (hi - lo) // 2
            yield from sort(lo, mid)
            yield from sort(mid + 1, hi)
            yield from merge(lo, hi, 1)

    return tuple(sort(0, PEER_TOPK - 1))


_SORT16 = _sort16_pairs()


def _exchange(vals, i, j):
    a, b = vals[i], vals[j]
    if b is None:
        return
    if a is None:
        vals[i], vals[j] = b, None
        return
    vals[i], vals[j] = jnp.maximum(a, b), jnp.minimum(a, b)


def _sort16_desc(vals):
    vals = list(vals)
    for i, j in _SORT16:
        _exchange(vals, i, j)
    return vals


def _merge_top16(a, b):
    vals = []
    for k in range(PEER_TOPK):
        x, y = a[k], b[PEER_TOPK - 1 - k]
        vals.append(y if x is None else (x if y is None else jnp.maximum(x, y)))
    stride = PEER_TOPK // 2
    while stride:
        for i in range(PEER_TOPK):
            if i & stride == 0:
                _exchange(vals, i, i + stride)
        stride //= 2
    return vals


def _top16_of_rows(s):
    vals = _sort16_desc([s[r * SUBLANES:(r + 1) * SUBLANES, :] for r in range(N_KEYS // SUBLANES)])
    shift = SUBLANES // 2
    while shift:
        vals = _merge_top16(vals, [pltpu.roll(v, shift, axis=0) for v in vals])
        shift //= 2
    return vals


_REST_PAIRS = tuple((k1, k2) for k1 in range(2, PEER_TOPK) for k2 in range(1, PEER_TOPK)
                    if (k1 + 1) * (k2 + 1) <= PEER_TOPK)
assert len(_REST_PAIRS) <= PEER_TOPK


def _top16_sums(v1, v2):
    pad = lambda lst: lst + [None] * (PEER_TOPK - len(lst))
    row0 = [v1[0] + v2[k] for k in range(PEER_TOPK)]
    col0 = pad([v1[k] + v2[0] for k in range(1, PEER_TOPK)])
    row1 = pad([v1[1] + v2[k] for k in range(1, PEER_TOPK) if 2 * (k + 1) <= PEER_TOPK])
    rest = _sort16_desc(pad([v1[a] + v2[b] for a, b in _REST_PAIRS]))
    return _merge_top16(_merge_top16(row0, col0), _merge_top16(row1, rest))


def _count_leading(vals, pred):
    g8 = pred(vals[7])
    g4 = pred(jnp.where(g8, vals[11], vals[3]))
    g2 = pred(jnp.where(g8, jnp.where(g4, vals[13], vals[9]), jnp.where(g4, vals[5], vals[1])))
    quads = [jnp.where(g2, vals[4 * m + 2], vals[4 * m]) for m in range(4)]
    g1 = pred(jnp.where(g8, jnp.where(g4, quads[3], quads[2]), jnp.where(g4, quads[1], quads[0])))
    g16 = pred(vals[15])
    cnt = (jnp.where(g8, jnp.where(g4, 12.0, 8.0), jnp.where(g4, 4.0, 0.0))
           + jnp.where(g2, jnp.where(g1, 3.0, 2.0), jnp.where(g1, 1.0, 0.0)))
    return jnp.where(g16, 16.0, cnt)


def _pair_words(x):
    bits = pltpu.bitcast(x.astype(BF16).astype(F32), jnp.uint32)
    return bits | (bits >> 16)


def _route_lane_block(s1_ref, s2_ref, a_ref, c_ref, b_ref, r_ref, h, lanes):
    s1 = s1_ref[:, lanes]
    s2 = s2_ref[:, lanes]
    v1 = _top16_of_rows(s1)
    v2 = _top16_of_rows(s2)
    top = _top16_sums(v1, v2)
    tau = top[PEER_TOPK - 1]
    z = jnp.ones_like(tau)
    for k in range(1, PEER_TOPK):
        z = z + jnp.exp(top[k] - top[0])
    half_inv_z = 0.5 / z
    for r in range(N_KEYS // SUBLANES):
        rs = slice(r * SUBLANES, (r + 1) * SUBLANES)
        x1, x2 = s1[rs, :], s2[rs, :]
        cnt = _count_leading(v2, lambda v: x1 + v >= tau)
        rank = _count_leading(v2, lambda v: v > x2)
        a_ref[h, rs, lanes] = _pair_words(jnp.exp(x1 - v1[0]) * half_inv_z)
        c_ref[h, rs, lanes] = _pair_words(cnt)
        b_ref[rs, lanes] = jnp.exp(x2 - v2[0])
        r_ref[rs, lanes] = rank


def _bf16_bits(x):
    shape = x.shape
    return pltpu.bitcast(x.reshape(-1, shape[-1]), jnp.int16).reshape(shape)


_GELU_C0 = float(np.sqrt(2.0 / np.pi))
_GELU_C1 = float(np.sqrt(2.0 / np.pi) * 0.044715)


def _twice_gelu_tanh(x):
    return x * (1.0 + jnp.tanh(x * (_GELU_C0 + _GELU_C1 * (x * x))))


def _peer_kernel(rows_per_step, n_prompt_tiles, xp_ref, xs_ref, gffn_ref, wq_ref, k1_ref, k2_ref, u_ref, vt_ref,
                 gfin_ref, yp_ref, ys_ref,
                 ht_ref, acc_ref, s1_ref, s2_ref, a_ref, c_ref, b_ref, r_ref, bb_ref, rb_ref,
                 act_ref, wact_ref):
    t = pl.program_id(0)
    e = pl.program_id(1)
    tokens = xp_ref.shape[0]
    n_lane_blocks = tokens // LANES
    groups = N_KEYS // BF16_ROWS

    def tile_input():
        return jnp.where(t < n_prompt_tiles, xp_ref[...], xs_ref[...])

    @pl.when(e == 0)
    def _route():
        h2 = _rms_rows(tile_input()) * gffn_ref[...]
        ht_ref[...] = h2.T.astype(BF16)
        acc_ref[...] = jnp.zeros_like(acc_ref)

        def head_body(h, carry):
            q0 = pl.multiple_of(h * 2 * D_KEY_HALF, 2 * D_KEY_HALF)
            qt = jnp.dot(wq_ref[pl.ds(q0, 2 * D_KEY_HALF), :], ht_ref[...], preferred_element_type=F32)
            s1_ref[...] = jnp.dot(k1_ref[h], qt[:D_KEY_HALF].astype(BF16), preferred_element_type=F32)
            s2_ref[...] = jnp.dot(k2_ref[h], qt[D_KEY_HALF:].astype(BF16), preferred_element_type=F32)

            def lane_body(lb, c2):
                lanes = pl.ds(pl.multiple_of(lb * LANES, LANES), LANES)
                _route_lane_block(s1_ref, s2_ref, a_ref, c_ref, b_ref, r_ref, h, lanes)
                return c2

            lax.fori_loop(0, n_lane_blocks, lane_body, 0)
            bb_ref[h] = b_ref[...].astype(BF16).reshape(groups, BF16_ROWS, tokens)
            rb_ref[h] = r_ref[...].astype(BF16).reshape(groups, BF16_ROWS, tokens)
            return carry

        lax.fori_loop(0, PEER_HEADS, head_body, 0)

    n_split = 2 if tokens % (2 * 2 * LANES) == 0 else 1
    width = tokens // n_split
    rows_half = UNIT_ROWS
    units = [(eh, sp) for eh in range(rows_per_step // UNIT_ROWS) for sp in range(n_split)]
    row0 = pl.multiple_of(e * rows_per_step, SUBLANES)
    zero = jnp.zeros((), BF16)

    def row_as_bf16(words_ref, h, ii, lanes):
        tile = words_ref[h, pl.ds(row0 + ii // SUBLANES * SUBLANES, SUBLANES), lanes]
        sub = ii % SUBLANES
        return pltpu.bitcast(jnp.broadcast_to(tile[sub:sub + 1, :], (SUBLANES, LANES)), BF16)

    def gate_rows(ii, lanes):
        w = None
        for h in range(PEER_HEADS):
            c16 = row_as_bf16(c_ref, h, ii, lanes)
            a16 = row_as_bf16(a_ref, h, ii, lanes)
            keep = _bf16_bits(rb_ref[h, :, :, lanes]) < _bf16_bits(c16)[None]
            term = jnp.where(keep, bb_ref[h, :, :, lanes], zero) * a16[None]
            w = term if w is None else w + term
        rs = slice(ii * N_KEYS, (ii + 1) * N_KEYS)
        act = _twice_gelu_tanh(act_ref[rs, lanes]).reshape(groups, BF16_ROWS, LANES)
        wact_ref[rs, lanes] = (w * act).reshape(N_KEYS, LANES)

    for eh, sp in units:
        cols = slice(sp * width, (sp + 1) * width)
        rs = slice(eh * rows_half * N_KEYS, (eh + 1) * rows_half * N_KEYS)
        act_ref[rs, cols] = jnp.dot(u_ref[rs, :], ht_ref[:, cols], preferred_element_type=F32).astype(BF16)
    for eh, sp in units:
        for lb in range(width // LANES):
            lane0 = sp * width + lb * LANES
            for ii in range(eh * rows_half, (eh + 1) * rows_half):
                gate_rows(ii, slice(lane0, lane0 + LANES))
    for eh, sp in units:
        cols = slice(sp * width, (sp + 1) * width)
        rs = slice(eh * rows_half * N_KEYS, (eh + 1) * rows_half * N_KEYS)
        acc_ref[:, cols] += jnp.dot(vt_ref[:, rs], wact_ref[rs, cols], preferred_element_type=F32)

    @pl.when(e == pl.num_programs(1) - 1)
    def _():
        out = tile_input() + acc_ref[...].T
        y = _rms_rows(out) * gfin_ref[...]

        @pl.when(t < n_prompt_tiles)
        def _():
            yp_ref[...] = y

        @pl.when(t >= n_prompt_tiles)
        def _():
            ys_ref[...] = y


def _peer(x_prompt, x_sample, gffn, wq_t, keys1, keys2, u_tab, v_tab_t, gfin):
    n_prompt_rows, d = x_prompt.shape
    n_sample_rows = x_sample.shape[0]
    total = n_prompt_rows + n_sample_rows
    n_experts = u_tab.shape[0]
    tokens = next(t for t in (512, 256, 128) if n_prompt_rows % t == 0 and n_sample_rows % t == 0)
    n_prompt_tiles = n_prompt_rows // tokens
    prompt_block = lambda t, e: (jnp.minimum(t, n_prompt_tiles - 1), 0)
    sample_block = lambda t, e: (jnp.maximum(t - n_prompt_tiles, 0), 0)
    rows_per_step = ROWS_PER_STEP
    experts_per_step = rows_per_step * N_KEYS
    groups = N_KEYS // BF16_ROWS
    head_shape = (PEER_HEADS, N_KEYS, tokens)
    return pl.pallas_call(
        functools.partial(_peer_kernel, rows_per_step, n_prompt_tiles),
        grid=(total // tokens, n_experts // experts_per_step),
        in_specs=[
            pl.BlockSpec((tokens, d), prompt_block),
            pl.BlockSpec((tokens, d), sample_block),
            _full((1, d)),
            _full(wq_t.shape), _full(keys1.shape), _full(keys2.shape),
            pl.BlockSpec((experts_per_step, d), lambda t, e: (e, 0)),
            pl.BlockSpec((d, experts_per_step), lambda t, e: (0, e)),
            _full((1, d)),
        ],
        out_specs=[pl.BlockSpec((tokens, d), prompt_block), pl.BlockSpec((tokens, d), sample_block)],
        out_shape=[
            jax.ShapeDtypeStruct((n_prompt_rows, d), F32),
            jax.ShapeDtypeStruct((n_sample_rows, d), F32),
        ],
        scratch_shapes=[
            pltpu.VMEM((d, tokens), BF16),
            pltpu.VMEM((d, tokens), F32),
            pltpu.VMEM((N_KEYS, tokens), F32), pltpu.VMEM((N_KEYS, tokens), F32),
            pltpu.VMEM(head_shape, jnp.uint32), pltpu.VMEM(head_shape, jnp.uint32),
            pltpu.VMEM((N_KEYS, tokens), F32), pltpu.VMEM((N_KEYS, tokens), F32),
            pltpu.VMEM((PEER_HEADS, groups, BF16_ROWS, tokens), BF16),
            pltpu.VMEM((PEER_HEADS, groups, BF16_ROWS, tokens), BF16),
            pltpu.VMEM((experts_per_step, tokens), BF16),
            pltpu.VMEM((experts_per_step, tokens), BF16),
        ],
        compiler_params=pltpu.CompilerParams(
            dimension_semantics=("arbitrary", "arbitrary"), vmem_limit_bytes=VMEM_LIMIT),
        name="peer",
    )(x_prompt, x_sample, gffn, wq_t, keys1, keys2, u_tab, v_tab_t, gfin)


def kernel(x_prompt, x_sample, state_ret, state_conv, norm_mix_g, w_in, conv_w, w_out, norm_ffn_g,
           peer_wq, peer_keys1, peer_keys2, peer_u, peer_v, norm_final_g):
    depth = w_in.shape[0]
    assert depth == 1
    batch, seq, d = x_prompt.shape
    dec_batch, dec_seq, _ = x_sample.shape

    gmix = norm_mix_g[0][None, :]
    gffn = norm_ffn_g[0][None, :]
    gfin = norm_final_g[None, :]
    w_in_b = w_in[0].astype(BF16)
    w_out_b = w_out[0].astype(BF16)
    wq_t = peer_wq[0].T.astype(BF16)
    keys1 = peer_keys1[0].astype(BF16)
    keys2 = peer_keys2[0].astype(BF16)
    u_tab = peer_u[0].astype(BF16)
    v_tab_t = peer_v[0].T.astype(BF16)

    xp, ret_p, conv_p = _mixer_prompt(x_prompt, gmix, w_in_b, conv_w[0], w_out_b)
    xs, ret_s, conv_s = _mixer_sample(x_sample, state_ret[0], state_conv[0], PAST_LEN,
                                      gmix, w_in_b, conv_w[0], w_out_b)
    y_prompt, y_sample = _peer(xp, xs, gffn, wq_t, keys1, keys2, u_tab, v_tab_t, gfin)
    return (y_prompt.reshape(batch, seq, d), y_sample.reshape(dec_batch, dec_seq, d),
            ret_p[None], conv_p[None], ret_s[None], conv_s[None])
```

```python
import functools

import numpy as np
import jax
import jax.numpy as jnp
from jax import lax
from jax.experimental import pallas as pl
from jax.experimental.pallas import tpu as pltpu

F32 = jnp.float32
BF16 = jnp.bfloat16

H_RET = 4
DK_RET = 128
D_RET = H_RET * DK_RET
D_CONV = 512
CONV_W = 3
CHUNK = 64
ROPE_THETA = 10000.0
RMS_EPS = 1e-6
PEER_HEADS = 8
N_KEYS = 128
PEER_TOPK = 16
D_KEY_HALF = 128
PAST_LEN = 4096

ROW_BLOCK = 128
LANES = 128
SUBLANES = 8
BF16_ROWS = 16
VMEM_LIMIT = 56 * 1024 * 1024
ROWS_PER_STEP = 16
UNIT_ROWS = 8


def _log_decay():
    return np.log1p(-np.exp2(-5.0 - np.arange(H_RET))).astype(np.float32).astype(np.float64)


def _retention_tables(chunk):
    lg = _log_decay()[:, None, None]
    idx = np.arange(ROW_BLOCK)
    ci, cj = idx[:, None] // chunk, idx[None, :] // chunk
    dist = (idx[:, None] - idx[None, :]).astype(np.float64)
    same = np.exp(lg * np.abs(dist)[None])
    later = np.exp(lg * dist[None])
    mask = np.where((ci == cj)[None], same, np.where((ci > cj)[None], later, 0.0))
    return mask.astype(np.float32)


def _prompt_tables():
    lg = _log_decay()
    idx = np.arange(ROW_BLOCK, dtype=np.float64)
    mask = _retention_tables(CHUNK)
    qd = np.exp(lg[:, None] * (idx[None] + 1.0))
    kd = np.exp(lg[:, None] * (ROW_BLOCK - 1.0 - idx[None]))
    cd = np.exp(lg * ROW_BLOCK)
    bc = lambda t: np.broadcast_to(t[:, :, None], (H_RET, ROW_BLOCK, DK_RET)).astype(np.float32)
    return mask, bc(qd), bc(kd), [float(c) for c in cd]


def _sample_tables(seq):
    lg = _log_decay()
    idx = np.arange(ROW_BLOCK)
    pos = (idx % seq).astype(np.float64)
    same_seq = (idx[:, None] // seq) == (idx[None, :] // seq)
    dist = np.abs(pos[:, None] - pos[None, :])
    mask = np.where(same_seq[None], np.exp(lg[:, None, None] * dist[None]), 0.0).astype(np.float32)
    qd = np.exp(lg[:, None] * (pos[None] + 1.0))
    kd = np.exp(lg[:, None] * (seq - 1.0 - pos[None]))
    cd = np.exp(lg * seq)
    bc = lambda t: np.broadcast_to(t[:, :, None], (H_RET, ROW_BLOCK, DK_RET)).astype(np.float32)
    return mask, bc(qd), bc(kd), [float(c) for c in cd]


def _rope_tables(pos):
    half = DK_RET // 2
    inv = ROPE_THETA ** (-jnp.arange(half, dtype=F32) / half)
    ang = pos.astype(F32)[:, None] * inv[None, :]
    cos, sin = jnp.cos(ang), jnp.sin(ang)
    return jnp.concatenate([cos, cos], axis=1), jnp.concatenate([-sin, sin], axis=1)


def _rms_rows(x):
    return x * lax.rsqrt(jnp.mean(x * x, axis=-1, keepdims=True) + RMS_EPS)


def _project_and_rotate(x, gmix_ref, w_in_ref, cos_ref, sin_ref, proj_ref):
    h = (_rms_rows(x) * gmix_ref[...]).astype(BF16)
    proj_ref[...] = jnp.dot(h, w_in_ref[...], preferred_element_type=F32)
    cs, sn = cos_ref[...], sin_ref[...]
    scale = DK_RET ** -0.5
    for hh in range(2 * H_RET):
        sl = slice(hh * DK_RET, (hh + 1) * DK_RET)
        t = proj_ref[:, sl]
        t = t * cs + pltpu.roll(t, DK_RET // 2, axis=1) * sn
        proj_ref[:, sl] = t * scale if hh < H_RET else t


def _head_cols(group, h):
    start = group * D_RET + h * DK_RET
    return slice(start, start + DK_RET)


def _gate_and_store(o, g, mix_ref, rows, h):
    o = _rms_rows(o)
    mix_ref[rows, _head_cols(0, h)] = (o * (g * (1.0 / (1.0 + jnp.exp(-g))))).astype(BF16)


def _short_conv(u, prev0, prev1, pos, w_ref):
    p1 = jnp.where(pos == 0, prev1, pltpu.roll(u, 1, axis=0))
    p2 = jnp.where(pos == 0, prev0, jnp.where(pos == 1, prev1, pltpu.roll(u, 2, axis=0)))
    return w_ref[0:1, :] * p2 + w_ref[1:2, :] * p1 + w_ref[2:3, :] * u


def _mixer_prompt_kernel(cd, x_ref, cos_ref, sin_ref, gmix_ref, w_in_ref, conv_w_ref, w_out_ref,
                         mask_ref, qd_ref, kd_ref,
                         out_ref, state_ref, conv_ref,
                         proj_ref, mix_ref):
    rows_total = x_ref.shape[0]

    @pl.when(pl.program_id(1) == 0)
    def _():
        state_ref[...] = jnp.zeros_like(state_ref)
        conv_ref[...] = jnp.zeros_like(conv_ref)

    x = x_ref[...]
    _project_and_rotate(x, gmix_ref, w_in_ref, cos_ref, sin_ref, proj_ref)

    for blk in range(rows_total // ROW_BLOCK):
        rows = slice(blk * ROW_BLOCK, (blk + 1) * ROW_BLOCK)
        for h in range(H_RET):
            q = proj_ref[rows, _head_cols(0, h)]
            k = proj_ref[rows, _head_cols(1, h)]
            vb = proj_ref[rows, _head_cols(2, h)].astype(BF16)
            a = lax.dot_general(q.astype(BF16), k.astype(BF16), (((1,), (1,)), ((), ())),
                                preferred_element_type=F32)
            s = state_ref[h]
            o = (jnp.dot((a * mask_ref[h]).astype(BF16), vb, preferred_element_type=F32)
                 + jnp.dot((q * qd_ref[h]).astype(BF16), s.astype(BF16), preferred_element_type=F32))
            kt = (k * kd_ref[h]).T.astype(BF16)
            state_ref[h] = cd[h] * s + jnp.dot(kt, vb, preferred_element_type=F32)
            _gate_and_store(o, proj_ref[rows, _head_cols(3, h)], mix_ref, rows, h)

    base = 4 * D_RET
    u = proj_ref[:, base + D_CONV:base + 2 * D_CONV] * proj_ref[:, base + 2 * D_CONV:base + 3 * D_CONV]
    pos = lax.broadcasted_iota(jnp.int32, u.shape, 0)
    y = _short_conv(u, conv_ref[0:1, :], conv_ref[1:2, :], pos, conv_w_ref)
    conv_ref[...] = u[rows_total - (CONV_W - 1):, :]
    mix_ref[:, D_RET:] = (proj_ref[:, base:base + D_CONV] * y).astype(BF16)

    out_ref[...] = x + jnp.dot(mix_ref[...], w_out_ref[...], preferred_element_type=F32)


def _mixer_sample_kernel(cd, seq, x_ref, cos_ref, sin_ref, gmix_ref, w_in_ref, conv_w_ref, w_out_ref,
                         mask_ref, qd_ref, kd_ref, state_in_ref, conv_in_ref,
                         out_ref, state_ref, conv_ref,
                         proj_ref, mix_ref):
    nseq = ROW_BLOCK // seq
    x = x_ref[...]
    _project_and_rotate(x, gmix_ref, w_in_ref, cos_ref, sin_ref, proj_ref)

    rows = slice(0, ROW_BLOCK)
    row_id = lax.broadcasted_iota(jnp.int32, (ROW_BLOCK, DK_RET), 0)
    for h in range(H_RET):
        q = proj_ref[rows, _head_cols(0, h)]
        k = proj_ref[rows, _head_cols(1, h)]
        vb = proj_ref[rows, _head_cols(2, h)].astype(BF16)
        a = lax.dot_general(q.astype(BF16), k.astype(BF16), (((1,), (1,)), ((), ())),
                            preferred_element_type=F32)
        qs = (q * qd_ref[h]).astype(BF16)
        ks = k * kd_ref[h]
        inter = []
        for b in range(nseq):
            s = state_in_ref[b, h]
            inter.append(jnp.dot(qs[b * seq:(b + 1) * seq, :], s.astype(BF16), preferred_element_type=F32))
            in_seq = (row_id >= b * seq) & (row_id < (b + 1) * seq)
            kt = jnp.where(in_seq, ks, 0.0).T.astype(BF16)
            state_ref[b, h] = cd[h] * s + jnp.dot(kt, vb, preferred_element_type=F32)
        o = (jnp.dot((a * mask_ref[h]).astype(BF16), vb, preferred_element_type=F32)
             + jnp.concatenate(inter, axis=0))
        _gate_and_store(o, proj_ref[rows, _head_cols(3, h)], mix_ref, rows, h)

    base = 4 * D_RET
    u = proj_ref[:, base + D_CONV:base + 2 * D_CONV] * proj_ref[:, base + 2 * D_CONV:base + 3 * D_CONV]
    pos = lax.broadcasted_iota(jnp.int32, u.shape, 0) % seq
    prev0 = jnp.concatenate([jnp.broadcast_to(conv_in_ref[b, 0:1, :], (seq, D_CONV)) for b in range(nseq)], axis=0)
    prev1 = jnp.concatenate([jnp.broadcast_to(conv_in_ref[b, 1:2, :], (seq, D_CONV)) for b in range(nseq)], axis=0)
    y = _short_conv(u, prev0, prev1, pos, conv_w_ref)
    for b in range(nseq):
        conv_ref[b] = u[(b + 1) * seq - (CONV_W - 1):(b + 1) * seq, :]
    mix_ref[:, D_RET:] = (proj_ref[:, base:base + D_CONV] * y).astype(BF16)

    out_ref[...] = x + jnp.dot(mix_ref[...], w_out_ref[...], preferred_element_type=F32)


def _full(shape):
    return pl.BlockSpec(shape, lambda *_: (0,) * len(shape))


def _mixer_prompt(x, gmix, w_in, conv_w, w_out):
    batch, seq, d = x.shape
    d_in = w_in.shape[1]
    rows = 512 if seq % 512 == 0 else ROW_BLOCK
    assert seq % rows == 0 and seq > CHUNK and ROW_BLOCK % CHUNK == 0
    tiles = seq // rows
    mask, qd, kd, cd = _prompt_tables()
    cos, sin = _rope_tables(jnp.arange(seq, dtype=jnp.int32))
    hshape = (H_RET, ROW_BLOCK, DK_RET)
    return pl.pallas_call(
        functools.partial(_mixer_prompt_kernel, cd),
        grid=(batch, tiles),
        in_specs=[
            pl.BlockSpec((None, rows, d), lambda b, l: (b, l, 0)),
            pl.BlockSpec((rows, DK_RET), lambda b, l: (l, 0)),
            pl.BlockSpec((rows, DK_RET), lambda b, l: (l, 0)),
            _full((1, d)), _full((d, d_in)), _full((CONV_W, D_CONV)), _full((d, d)),
            _full(hshape), _full(hshape), _full(hshape),
        ],
        out_specs=[
            pl.BlockSpec((rows, d), lambda b, l: (b * tiles + l, 0)),
            pl.BlockSpec((None, H_RET, DK_RET, DK_RET), lambda b, l: (b, 0, 0, 0)),
            pl.BlockSpec((None, CONV_W - 1, D_CONV), lambda b, l: (b, 0, 0)),
        ],
        out_shape=[
            jax.ShapeDtypeStruct((batch * seq, d), F32),
            jax.ShapeDtypeStruct((batch, H_RET, DK_RET, DK_RET), F32),
            jax.ShapeDtypeStruct((batch, CONV_W - 1, D_CONV), F32),
        ],
        scratch_shapes=[pltpu.VMEM((rows, d_in), F32), pltpu.VMEM((rows, d), BF16)],
        compiler_params=pltpu.CompilerParams(
            dimension_semantics=("parallel", "arbitrary"), vmem_limit_bytes=VMEM_LIMIT),
        name="mixer_prompt",
    )(x, cos, sin, gmix, w_in, conv_w, w_out, jnp.asarray(mask), jnp.asarray(qd), jnp.asarray(kd))


def _mixer_sample(x, state, conv_state, past_len, gmix, w_in, conv_w, w_out):
    batch, seq, d = x.shape
    d_in = w_in.shape[1]
    assert seq <= CHUNK and ROW_BLOCK % seq == 0 and seq >= CONV_W - 1
    nseq = ROW_BLOCK // seq
    assert batch % nseq == 0
    mask, qd, kd, cd = _sample_tables(seq)
    cos, sin = _rope_tables(past_len + jnp.tile(jnp.arange(seq, dtype=jnp.int32), nseq))
    hshape = (H_RET, ROW_BLOCK, DK_RET)
    x2 = x.reshape(batch * seq, d)
    return pl.pallas_call(
        functools.partial(_mixer_sample_kernel, cd, seq),
        grid=(batch // nseq,),
        in_specs=[
            pl.BlockSpec((ROW_BLOCK, d), lambda g: (g, 0)),
            _full((ROW_BLOCK, DK_RET)), _full((ROW_BLOCK, DK_RET)),
            _full((1, d)), _full((d, d_in)), _full((CONV_W, D_CONV)), _full((d, d)),
            _full(hshape), _full(hshape), _full(hshape),
            pl.BlockSpec((nseq, H_RET, DK_RET, DK_RET), lambda g: (g, 0, 0, 0)),
            pl.BlockSpec((nseq, CONV_W - 1, D_CONV), lambda g: (g, 0, 0)),
        ],
        out_specs=[
            pl.BlockSpec((ROW_BLOCK, d), lambda g: (g, 0)),
            pl.BlockSpec((nseq, H_RET, DK_RET, DK_RET), lambda g: (g, 0, 0, 0)),
            pl.BlockSpec((nseq, CONV_W - 1, D_CONV), lambda g: (g, 0, 0)),
        ],
        out_shape=[
            jax.ShapeDtypeStruct((batch * seq, d), F32),
            jax.ShapeDtypeStruct((batch, H_RET, DK_RET, DK_RET), F32),
            jax.ShapeDtypeStruct((batch, CONV_W - 1, D_CONV), F32),
        ],
        scratch_shapes=[pltpu.VMEM((ROW_BLOCK, d_in), F32), pltpu.VMEM((ROW_BLOCK, d), BF16)],
        compiler_params=pltpu.CompilerParams(
            dimension_semantics=("parallel",), vmem_limit_bytes=VMEM_LIMIT),
        name="mixer_sample",
    )(x2, cos, sin, gmix, w_in, conv_w, w_out, jnp.asarray(mask), jnp.asarray(qd), jnp.asarray(kd),
      state, conv_state)


def _sort16_pairs():
    def merge(lo, hi, r):
        step = r * 2
        if step < hi - lo:
            yield from merge(lo, hi, step)
            yield from merge(lo + r, hi, step)
            for i in range(lo + r, hi - r, step):
                yield (i, i + r)
        else:
            yield (lo, lo + r)

    def sort(lo, hi):
        if hi - lo >= 1:
            mid = lo + (hi - lo) // 2
            yield from sort(lo, mid)
            yield from sort(mid + 1, hi)
            yield from merge(lo, hi, 1)

    return tuple(sort(0, PEER_TOPK - 1))


_SORT16 = _sort16_pairs()


def _exchange(vals, i, j):
    a, b = vals[i], vals[j]
    if b is None:
        return
    if a is None:
        vals[i], vals[j] = b, None
        return
    vals[i], vals[j] = jnp.maximum(a, b), jnp.minimum(a, b)


def _sort16_desc(vals):
    vals = list(vals)
    for i, j in _SORT16:
        _exchange(vals, i, j)
    return vals


def _merge_top16(a, b):
    vals = []
    for k in range(PEER_TOPK):
        x, y = a[k], b[PEER_TOPK - 1 - k]
        vals.append(y if x is None else (x if y is None else jnp.maximum(x, y)))
    stride = PEER_TOPK // 2
    while stride:
        for i in range(PEER_TOPK):
            if i & stride == 0:
                _exchange(vals, i, i + stride)
        stride //= 2
    return vals


def _top16_of_rows(s):
    vals = _sort16_desc([s[r * SUBLANES:(r + 1) * SUBLANES, :] for r in range(N_KEYS // SUBLANES)])
    shift = SUBLANES // 2
    while shift:
        vals = _merge_top16(vals, [pltpu.roll(v, shift, axis=0) for v in vals])
        shift //= 2
    return vals


_REST_PAIRS = tuple((k1, k2) for k1 in range(2, PEER_TOPK) for k2 in range(1, PEER_TOPK)
                    if (k1 + 1) * (k2 + 1) <= PEER_TOPK)
assert len(_REST_PAIRS) <= PEER_TOPK


def _top16_sums(v1, v2):
    pad = lambda lst: lst + [None] * (PEER_TOPK - len(lst))
    row0 = [v1[0] + v2[k] for k in range(PEER_TOPK)]
    col0 = pad([v1[k] + v2[0] for k in range(1, PEER_TOPK)])
    row1 = pad([v1[1] + v2[k] for k in range(1, PEER_TOPK) if 2 * (k + 1) <= PEER_TOPK])
    rest = _sort16_desc(pad([v1[a] + v2[b] for a, b in _REST_PAIRS]))
    return _merge_top16(_merge_top16(row0, col0), _merge_top16(row1, rest))


def _count_leading(vals, pred):
    g8 = pred(vals[7])
    g4 = pred(jnp.where(g8, vals[11], vals[3]))
    g2 = pred(jnp.where(g8, jnp.where(g4, vals[13], vals[9]), jnp.where(g4, vals[5], vals[1])))
    quads = [jnp.where(g2, vals[4 * m + 2], vals[4 * m]) for m in range(4)]
    g1 = pred(jnp.where(g8, jnp.where(g4, quads[3], quads[2]), jnp.where(g4, quads[1], quads[0])))
    g16 = pred(vals[15])
    cnt = (jnp.where(g8, jnp.where(g4, 12.0, 8.0), jnp.where(g4, 4.0, 0.0))
           + jnp.where(g2, jnp.where(g1, 3.0, 2.0), jnp.where(g1, 1.0, 0.0)))
    return jnp.where(g16, 16.0, cnt)


def _pair_words(x):
    bits = pltpu.bitcast(x.astype(BF16).astype(F32), jnp.uint32)
    return bits | (bits >> 16)


def _route_lane_block(s1_ref, s2_ref, a_ref, c_ref, b_ref, r_ref, h, lanes):
    s1 = s1_ref[:, lanes]
    s2 = s2_ref[:, lanes]
    v1 = _top16_of_rows(s1)
    v2 = _top16_of_rows(s2)
    top = _top16_sums(v1, v2)
    tau = top[PEER_TOPK - 1]
    z = jnp.ones_like(tau)
    for k in range(1, PEER_TOPK):
        z = z + jnp.exp(top[k] - top[0])
    half_inv_z = 0.5 / z
    for r in range(N_KEYS // SUBLANES):
        rs = slice(r * SUBLANES, (r + 1) * SUBLANES)
        x1, x2 = s1[rs, :], s2[rs, :]
        cnt = _count_leading(v2, lambda v: x1 + v >= tau)
        rank = _count_leading(v2, lambda v: v > x2)
        a_ref[h, rs, lanes] = _pair_words(jnp.exp(x1 - v1[0]) * half_inv_z)
        c_ref[h, rs, lanes] = _pair_words(cnt)
        b_ref[rs, lanes] = jnp.exp(x2 - v2[0])
        r_ref[rs, lanes] = rank


def _bf16_bits(x):
    shape = x.shape
    return pltpu.bitcast(x.reshape(-1, shape[-1]), jnp.int16).reshape(shape)


_GELU_C0 = float(np.sqrt(2.0 / np.pi))
_GELU_C1 = float(np.sqrt(2.0 / np.pi) * 0.044715)


def _twice_gelu_tanh(x):
    return x * (1.0 + jnp.tanh(x * (_GELU_C0 + _GELU_C1 * (x * x))))


def _peer_kernel(rows_per_step, n_prompt_tiles, xp_ref, xs_ref, gffn_ref, wq_ref, k1_ref, k2_ref, u_ref, vt_ref,
                 gfin_ref, yp_ref, ys_ref,
                 ht_ref, acc_ref, s1_ref, s2_ref, a_ref, c_ref, b_ref, r_ref, bb_ref, rb_ref,
                 act_ref, wact_ref):
    t = pl.program_id(0)
    e = pl.program_id(1)
    tokens = xp_ref.shape[0]
    n_lane_blocks = tokens // LANES
    groups = N_KEYS // BF16_ROWS

    def tile_input():
        return jnp.where(t < n_prompt_tiles, xp_ref[...], xs_ref[...])

    @pl.when(e == 0)
    def _route():
        h2 = _rms_rows(tile_input()) * gffn_ref[...]
        ht_ref[...] = h2.T.astype(BF16)
        acc_ref[...] = jnp.zeros_like(acc_ref)

        def head_body(h, carry):
            q0 = pl.multiple_of(h * 2 * D_KEY_HALF, 2 * D_KEY_HALF)
            qt = jnp.dot(wq_ref[pl.ds(q0, 2 * D_KEY_HALF), :], ht_ref[...], preferred_element_type=F32)
            s1_ref[...] = jnp.dot(k1_ref[h], qt[:D_KEY_HALF].astype(BF16), preferred_element_type=F32)
            s2_ref[...] = jnp.dot(k2_ref[h], qt[D_KEY_HALF:].astype(BF16), preferred_element_type=F32)

            def lane_body(lb, c2):
                lanes = pl.ds(pl.multiple_of(lb * LANES, LANES), LANES)
                _route_lane_block(s1_ref, s2_ref, a_ref, c_ref, b_ref, r_ref, h, lanes)
                return c2

            lax.fori_loop(0, n_lane_blocks, lane_body, 0)
            bb_ref[h] = b_ref[...].astype(BF16).reshape(groups, BF16_ROWS, tokens)
            rb_ref[h] = r_ref[...].astype(BF16).reshape(groups, BF16_ROWS, tokens)
            return carry

        lax.fori_loop(0, PEER_HEADS, head_body, 0)

    n_split = 2 if tokens % (2 * 2 * LANES) == 0 else 1
    width = tokens // n_split
    rows_half = UNIT_ROWS
    units = [(eh, sp) for eh in range(rows_per_step // UNIT_ROWS) for sp in range(n_split)]
    row0 = pl.multiple_of(e * rows_per_step, SUBLANES)
    zero = jnp.zeros((), BF16)

    def row_as_bf16(words_ref, h, ii, lanes):
        tile = words_ref[h, pl.ds(row0 + ii // SUBLANES * SUBLANES, SUBLANES), lanes]
        sub = ii % SUBLANES
        return pltpu.bitcast(jnp.broadcast_to(tile[sub:sub + 1, :], (SUBLANES, LANES)), BF16)

    def gate_rows(ii, lanes):
        w = None
        for h in range(PEER_HEADS):
            c16 = row_as_bf16(c_ref, h, ii, lanes)
            a16 = row_as_bf16(a_ref, h, ii, lanes)
            keep = _bf16_bits(rb_ref[h, :, :, lanes]) < _bf16_bits(c16)[None]
            term = jnp.where(keep, bb_ref[h, :, :, lanes], zero) * a16[None]
            w = term if w is None else w + term
        rs = slice(ii * N_KEYS, (ii + 1) * N_KEYS)
        act = _twice_gelu_tanh(act_ref[rs, lanes]).astype(BF16).reshape(groups, BF16_ROWS, LANES)
        wact_ref[rs, lanes] = (w * act).reshape(N_KEYS, LANES)

    for eh, sp in units:
        cols = slice(sp * width, (sp + 1) * width)
        rs = slice(eh * rows_half * N_KEYS, (eh + 1) * rows_half * N_KEYS)
        act_ref[rs, cols] = jnp.dot(u_ref[rs, :], ht_ref[:, cols], preferred_element_type=F32)
    for eh, sp in units:
        for lb in range(width // LANES):
            lane0 = sp * width + lb * LANES
            for ii in range(eh * rows_half, (eh + 1) * rows_half):
                gate_rows(ii, slice(lane0, lane0 + LANES))
    for eh, sp in units:
        cols = slice(sp * width, (sp + 1) * width)
        rs = slice(eh * rows_half * N_KEYS, (eh + 1) * rows_half * N_KEYS)
        acc_ref[:, cols] += jnp.dot(vt_ref[:, rs], wact_ref[rs, cols], preferred_element_type=F32)

    @pl.when(e == pl.num_programs(1) - 1)
    def _():
        out = tile_input() + acc_ref[...].T
        y = _rms_rows(out) * gfin_ref[...]

        @pl.when(t < n_prompt_tiles)
        def _():
            yp_ref[...] = y

        @pl.when(t >= n_prompt_tiles)
        def _():
            ys_ref[...] = y


def _peer(x_prompt, x_sample, gffn, wq_t, keys1, keys2, u_tab, v_tab_t, gfin):
    n_prompt_rows, d = x_prompt.shape
    n_sample_rows = x_sample.shape[0]
    total = n_prompt_rows + n_sample_rows
    n_experts = u_tab.shape[0]
    tokens = next(t for t in (512, 256, 128) if n_prompt_rows % t == 0 and n_sample_rows % t == 0)
    n_prompt_tiles = n_prompt_rows // tokens
    prompt_block = lambda t, e: (jnp.minimum(t, n_prompt_tiles - 1), 0)
    sample_block = lambda t, e: (jnp.maximum(t - n_prompt_tiles, 0), 0)
    rows_per_step = ROWS_PER_STEP
    experts_per_step = rows_per_step * N_KEYS
    groups = N_KEYS // BF16_ROWS
    head_shape = (PEER_HEADS, N_KEYS, tokens)
    return pl.pallas_call(
        functools.partial(_peer_kernel, rows_per_step, n_prompt_tiles),
        grid=(total // tokens, n_experts // experts_per_step),
        in_specs=[
            pl.BlockSpec((tokens, d), prompt_block),
            pl.BlockSpec((tokens, d), sample_block),
            _full((1, d)),
            _full(wq_t.shape), _full(keys1.shape), _full(keys2.shape),
            pl.BlockSpec((experts_per_step, d), lambda t, e: (e, 0)),
            pl.BlockSpec((d, experts_per_step), lambda t, e: (0, e)),
            _full((1, d)),
        ],
        out_specs=[pl.BlockSpec((tokens, d), prompt_block), pl.BlockSpec((tokens, d), sample_block)],
        out_shape=[
            jax.ShapeDtypeStruct((n_prompt_rows, d), F32),
            jax.ShapeDtypeStruct((n_sample_rows, d), F32),
        ],
        scratch_shapes=[
            pltpu.VMEM((d, tokens), BF16),
            pltpu.VMEM((d, tokens), F32),
            pltpu.VMEM((N_KEYS, tokens), F32), pltpu.VMEM((N_KEYS, tokens), F32),
            pltpu.VMEM(head_shape, jnp.uint32), pltpu.VMEM(head_shape, jnp.uint32),
            pltpu.VMEM((N_KEYS, tokens), F32), pltpu.VMEM((N_KEYS, tokens), F32),
            pltpu.VMEM((PEER_HEADS, groups, BF16_ROWS, tokens), BF16),
            pltpu.VMEM((PEER_HEADS, groups, BF16_ROWS, tokens), BF16),
            pltpu.VMEM((experts_per_step, tokens), F32),
            pltpu.VMEM((experts_per_step, tokens), BF16),
        ],
        compiler_params=pltpu.CompilerParams(
            dimension_semantics=("arbitrary", "arbitrary"), vmem_limit_bytes=VMEM_LIMIT),
        name="peer",
    )(x_prompt, x_sample, gffn, wq_t, keys1, keys2, u_tab, v_tab_t, gfin)


def kernel(x_prompt, x_sample, state_ret, state_conv, norm_mix_g, w_in, conv_w, w_out, norm_ffn_g,
           peer_wq, peer_keys1, peer_keys2, peer_u, peer_v, norm_final_g):
    depth = w_in.shape[0]
    assert depth == 1
    batch, seq, d = x_prompt.shape
    dec_batch, dec_seq, _ = x_sample.shape

    gmix = norm_mix_g[0][None, :]
    gffn = norm_ffn_g[0][None, :]
    gfin = norm_final_g[None, :]
    w_in_b = w_in[0].astype(BF16)
    w_out_b = w_out[0].astype(BF16)
    wq_t = peer_wq[0].T.astype(BF16)
    keys1 = peer_keys1[0].astype(BF16)
    keys2 = peer_keys2[0].astype(BF16)
    u_tab = peer_u[0].astype(BF16)
    v_tab_t = peer_v[0].T.astype(BF16)

    xp, ret_p, conv_p = _mixer_prompt(x_prompt, gmix, w_in_b, conv_w[0], w_out_b)
    xs, ret_s, conv_s = _mixer_sample(x_sample, state_ret[0], state_conv[0], PAST_LEN,
                                      gmix, w_in_b, conv_w[0], w_out_b)
    y_prompt, y_sample = _peer(xp, xs, gffn, wq_t, keys1, keys2, u_tab, v_tab_t, gfin)
    return (y_prompt.reshape(batch, seq, d), y_sample.reshape(dec_batch, dec_seq, d),
            ret_p[None], conv_p[None], ret_s[None], conv_s[None])
```

```python
import functools

import numpy as np
import jax
import jax.numpy as jnp
from jax import lax
from jax.experimental import pallas as pl
from jax.experimental.pallas import tpu as pltpu

F32 = jnp.float32
BF16 = jnp.bfloat16

H_RET = 4
DK_RET = 128
D_RET = H_RET * DK_RET
D_CONV = 512
CONV_W = 3
CHUNK = 64
ROPE_THETA = 10000.0
RMS_EPS = 1e-6
PEER_HEADS = 8
N_KEYS = 128
PEER_TOPK = 16
D_KEY_HALF = 128
PAST_LEN = 4096

ROW_BLOCK = 128
LANES = 128
SUBLANES = 8
BF16_ROWS = 16
VMEM_LIMIT = 56 * 1024 * 1024
ROWS_PER_STEP = 16
UNIT_ROWS = 8


def _log_decay():
    return np.log1p(-np.exp2(-5.0 - np.arange(H_RET))).astype(np.float32).astype(np.float64)


def _retention_tables(chunk):
    lg = _log_decay()[:, None, None]
    idx = np.arange(ROW_BLOCK)
    ci, cj = idx[:, None] // chunk, idx[None, :] // chunk
    dist = (idx[:, None] - idx[None, :]).astype(np.float64)
    same = np.exp(lg * np.abs(dist)[None])
    later = np.exp(lg * dist[None])
    mask = np.where((ci == cj)[None], same, np.where((ci > cj)[None], later, 0.0))
    return mask.astype(np.float32)


def _prompt_tables():
    lg = _log_decay()
    idx = np.arange(ROW_BLOCK, dtype=np.float64)
    mask = _retention_tables(CHUNK)
    qd = np.exp(lg[:, None] * (idx[None] + 1.0))
    kd = np.exp(lg[:, None] * (ROW_BLOCK - 1.0 - idx[None]))
    cd = np.exp(lg * ROW_BLOCK)
    bc = lambda t: np.broadcast_to(t[:, :, None], (H_RET, ROW_BLOCK, DK_RET)).astype(np.float32)
    return mask, bc(qd), bc(kd), [float(c) for c in cd]


def _sample_tables(seq):
    lg = _log_decay()
    idx = np.arange(ROW_BLOCK)
    pos = (idx % seq).astype(np.float64)
    same_seq = (idx[:, None] // seq) == (idx[None, :] // seq)
    dist = np.abs(pos[:, None] - pos[None, :])
    mask = np.where(same_seq[None], np.exp(lg[:, None, None] * dist[None]), 0.0).astype(np.float32)
    qd = np.exp(lg[:, None] * (pos[None] + 1.0))
    kd = np.exp(lg[:, None] * (seq - 1.0 - pos[None]))
    cd = np.exp(lg * seq)
    bc = lambda t: np.broadcast_to(t[:, :, None], (H_RET, ROW_BLOCK, DK_RET)).astype(np.float32)
    return mask, bc(qd), bc(kd), [float(c) for c in cd]


def _rope_tables(pos):
    half = DK_RET // 2
    inv = ROPE_THETA ** (-jnp.arange(half, dtype=F32) / half)
    ang = pos.astype(F32)[:, None] * inv[None, :]
    cos, sin = jnp.cos(ang), jnp.sin(ang)
    return jnp.concatenate([cos, cos], axis=1), jnp.concatenate([-sin, sin], axis=1)


def _rms_rows(x):
    return x * lax.rsqrt(jnp.mean(x * x, axis=-1, keepdims=True) + RMS_EPS)


def _project_and_rotate(x, gmix_ref, w_in_ref, cos_ref, sin_ref, proj_ref):
    h = (_rms_rows(x) * gmix_ref[...]).astype(BF16)
    proj_ref[...] = jnp.dot(h, w_in_ref[...], preferred_element_type=F32)
    cs, sn = cos_ref[...], sin_ref[...]
    scale = DK_RET ** -0.5
    for hh in range(2 * H_RET):
        sl = slice(hh * DK_RET, (hh + 1) * DK_RET)
        t = proj_ref[:, sl]
        t = t * cs + pltpu.roll(t, DK_RET // 2, axis=1) * sn
        proj_ref[:, sl] = t * scale if hh < H_RET else t


def _head_cols(group, h):
    start = group * D_RET + h * DK_RET
    return slice(start, start + DK_RET)


def _gate_and_store(o, g, mix_ref, rows, h):
    o = _rms_rows(o)
    mix_ref[rows, _head_cols(0, h)] = (o * (g * (1.0 / (1.0 + jnp.exp(-g))))).astype(BF16)


def _short_conv(u, prev0, prev1, pos, w_ref):
    p1 = jnp.where(pos == 0, prev1, pltpu.roll(u, 1, axis=0))
    p2 = jnp.where(pos == 0, prev0, jnp.where(pos == 1, prev1, pltpu.roll(u, 2, axis=0)))
    return w_ref[0:1, :] * p2 + w_ref[1:2, :] * p1 + w_ref[2:3, :] * u


def _mixer_prompt_kernel(cd, x_ref, cos_ref, sin_ref, gmix_ref, w_in_ref, conv_w_ref, w_out_ref,
                         mask_ref, qd_ref, kd_ref,
                         out_ref, state_ref, conv_ref,
                         proj_ref, mix_ref):
    rows_total = x_ref.shape[0]

    @pl.when(pl.program_id(1) == 0)
    def _():
        state_ref[...] = jnp.zeros_like(state_ref)
        conv_ref[...] = jnp.zeros_like(conv_ref)

    x = x_ref[...]
    _project_and_rotate(x, gmix_ref, w_in_ref, cos_ref, sin_ref, proj_ref)

    for blk in range(rows_total // ROW_BLOCK):
        rows = slice(blk * ROW_BLOCK, (blk + 1) * ROW_BLOCK)
        for h in range(H_RET):
            q = proj_ref[rows, _head_cols(0, h)]
            k = proj_ref[rows, _head_cols(1, h)]
            vb = proj_ref[rows, _head_cols(2, h)].astype(BF16)
            a = lax.dot_general(q.astype(BF16), k.astype(BF16), (((1,), (1,)), ((), ())),
                                preferred_element_type=F32)
            s = state_ref[h]
            o = (jnp.dot((a * mask_ref[h]).astype(BF16), vb, preferred_element_type=F32)
                 + jnp.dot((q * qd_ref[h]).astype(BF16), s.astype(BF16), preferred_element_type=F32))
            kt = (k * kd_ref[h]).T.astype(BF16)
            state_ref[h] = cd[h] * s + jnp.dot(kt, vb, preferred_element_type=F32)
            _gate_and_store(o, proj_ref[rows, _head_cols(3, h)], mix_ref, rows, h)

    base = 4 * D_RET
    u = proj_ref[:, base + D_CONV:base + 2 * D_CONV] * proj_ref[:, base + 2 * D_CONV:base + 3 * D_CONV]
    pos = lax.broadcasted_iota(jnp.int32, u.shape, 0)
    y = _short_conv(u, conv_ref[0:1, :], conv_ref[1:2, :], pos, conv_w_ref)
    conv_ref[...] = u[rows_total - (CONV_W - 1):, :]
    mix_ref[:, D_RET:] = (proj_ref[:, base:base + D_CONV] * y).astype(BF16)

    out_ref[...] = x + jnp.dot(mix_ref[...], w_out_ref[...], preferred_element_type=F32)


def _mixer_sample_kernel(cd, seq, x_ref, cos_ref, sin_ref, gmix_ref, w_in_ref, conv_w_ref, w_out_ref,
                         mask_ref, qd_ref, kd_ref, state_in_ref, conv_in_ref,
                         out_ref, state_ref, conv_ref,
                         proj_ref, mix_ref):
    nseq = ROW_BLOCK // seq
    x = x_ref[...]
    _project_and_rotate(x, gmix_ref, w_in_ref, cos_ref, sin_ref, proj_ref)

    rows = slice(0, ROW_BLOCK)
    row_id = lax.broadcasted_iota(jnp.int32, (ROW_BLOCK, DK_RET), 0)
    for h in range(H_RET):
        q = proj_ref[rows, _head_cols(0, h)]
        k = proj_ref[rows, _head_cols(1, h)]
        vb = proj_ref[rows, _head_cols(2, h)].astype(BF16)
        a = lax.dot_general(q.astype(BF16), k.astype(BF16), (((1,), (1,)), ((), ())),
                            preferred_element_type=F32)
        qs = (q * qd_ref[h]).astype(BF16)
        ks = k * kd_ref[h]
        inter = []
        for b in range(nseq):
            s = state_in_ref[b, h]
            inter.append(jnp.dot(qs[b * seq:(b + 1) * seq, :], s.astype(BF16), preferred_element_type=F32))
            in_seq = (row_id >= b * seq) & (row_id < (b + 1) * seq)
            kt = jnp.where(in_seq, ks, 0.0).T.astype(BF16)
            state_ref[b, h] = cd[h] * s + jnp.dot(kt, vb, preferred_element_type=F32)
        o = (jnp.dot((a * mask_ref[h]).astype(BF16), vb, preferred_element_type=F32)
             + jnp.concatenate(inter, axis=0))
        _gate_and_store(o, proj_ref[rows, _head_cols(3, h)], mix_ref, rows, h)

    base = 4 * D_RET
    u = proj_ref[:, base + D_CONV:base + 2 * D_CONV] * proj_ref[:, base + 2 * D_CONV:base + 3 * D_CONV]
    pos = lax.broadcasted_iota(jnp.int32, u.shape, 0) % seq
    prev0 = jnp.concatenate([jnp.broadcast_to(conv_in_ref[b, 0:1, :], (seq, D_CONV)) for b in range(nseq)], axis=0)
    prev1 = jnp.concatenate([jnp.broadcast_to(conv_in_ref[b, 1:2, :], (seq, D_CONV)) for b in range(nseq)], axis=0)
    y = _short_conv(u, prev0, prev1, pos, conv_w_ref)
    for b in range(nseq):
        conv_ref[b] = u[(b + 1) * seq - (CONV_W - 1):(b + 1) * seq, :]
    mix_ref[:, D_RET:] = (proj_ref[:, base:base + D_CONV] * y).astype(BF16)

    out_ref[...] = x + jnp.dot(mix_ref[...], w_out_ref[...], preferred_element_type=F32)


def _full(shape):
    return pl.BlockSpec(shape, lambda *_: (0,) * len(shape))


def _mixer_prompt(x, gmix, w_in, conv_w, w_out):
    batch, seq, d = x.shape
    d_in = w_in.shape[1]
    rows = 512 if seq % 512 == 0 else ROW_BLOCK
    assert seq % rows == 0 and seq > CHUNK and ROW_BLOCK % CHUNK == 0
    tiles = seq // rows
    mask, qd, kd, cd = _prompt_tables()
    cos, sin = _rope_tables(jnp.arange(seq, dtype=jnp.int32))
    hshape = (H_RET, ROW_BLOCK, DK_RET)
    return pl.pallas_call(
        functools.partial(_mixer_prompt_kernel, cd),
        grid=(batch, tiles),
        in_specs=[
            pl.BlockSpec((None, rows, d), lambda b, l: (b, l, 0)),
            pl.BlockSpec((rows, DK_RET), lambda b, l: (l, 0)),
            pl.BlockSpec((rows, DK_RET), lambda b, l: (l, 0)),
            _full((1, d)), _full((d, d_in)), _full((CONV_W, D_CONV)), _full((d, d)),
            _full(hshape), _full(hshape), _full(hshape),
        ],
        out_specs=[
            pl.BlockSpec((rows, d), lambda b, l: (b * tiles + l, 0)),
            pl.BlockSpec((None, H_RET, DK_RET, DK_RET), lambda b, l: (b, 0, 0, 0)),
            pl.BlockSpec((None, CONV_W - 1, D_CONV), lambda b, l: (b, 0, 0)),
        ],
        out_shape=[
            jax.ShapeDtypeStruct((batch * seq, d), F32),
            jax.ShapeDtypeStruct((batch, H_RET, DK_RET, DK_RET), F32),
            jax.ShapeDtypeStruct((batch, CONV_W - 1, D_CONV), F32),
        ],
        scratch_shapes=[pltpu.VMEM((rows, d_in), F32), pltpu.VMEM((rows, d), BF16)],
        compiler_params=pltpu.CompilerParams(
            dimension_semantics=("parallel", "arbitrary"), vmem_limit_bytes=VMEM_LIMIT),
        name="mixer_prompt",
    )(x, cos, sin, gmix, w_in, conv_w, w_out, jnp.asarray(mask), jnp.asarray(qd), jnp.asarray(kd))


def _mixer_sample(x, state, conv_state, past_len, gmix, w_in, conv_w, w_out):
    batch, seq, d = x.shape
    d_in = w_in.shape[1]
    assert seq <= CHUNK and ROW_BLOCK % seq == 0 and seq >= CONV_W - 1
    nseq = ROW_BLOCK // seq
    assert batch % nseq == 0
    mask, qd, kd, cd = _sample_tables(seq)
    cos, sin = _rope_tables(past_len + jnp.tile(jnp.arange(seq, dtype=jnp.int32), nseq))
    hshape = (H_RET, ROW_BLOCK, DK_RET)
    x2 = x.reshape(batch * seq, d)
    return pl.pallas_call(
        functools.partial(_mixer_sample_kernel, cd, seq),
        grid=(batch // nseq,),
        in_specs=[
            pl.BlockSpec((ROW_BLOCK, d), lambda g: (g, 0)),
            _full((ROW_BLOCK, DK_RET)), _full((ROW_BLOCK, DK_RET)),
            _full((1, d)), _full((d, d_in)), _full((CONV_W, D_CONV)), _full((d, d)),
            _full(hshape), _full(hshape), _full(hshape),
            pl.BlockSpec((nseq, H_RET, DK_RET, DK_RET), lambda g: (g, 0, 0, 0)),
            pl.BlockSpec((nseq, CONV_W - 1, D_CONV), lambda g: (g, 0, 0)),
        ],
        out_specs=[
            pl.BlockSpec((ROW_BLOCK, d), lambda g: (g, 0)),
            pl.BlockSpec((nseq, H_RET, DK_RET, DK_RET), lambda g: (g, 0, 0, 0)),
            pl.BlockSpec((nseq, CONV_W - 1, D_CONV), lambda g: (g, 0, 0)),
        ],
        out_shape=[
            jax.ShapeDtypeStruct((batch * seq, d), F32),
            jax.ShapeDtypeStruct((batch, H_RET, DK_RET, DK_RET), F32),
            jax.ShapeDtypeStruct((batch, CONV_W - 1, D_CONV), F32),
        ],
        scratch_shapes=[pltpu.VMEM((ROW_BLOCK, d_in), F32), pltpu.VMEM((ROW_BLOCK, d), BF16)],
        compiler_params=pltpu.CompilerParams(
            dimension_semantics=("parallel",), vmem_limit_bytes=VMEM_LIMIT),
        name="mixer_sample",
    )(x2, cos, sin, gmix, w_in, conv_w, w_out, jnp.asarray(mask), jnp.asarray(qd), jnp.asarray(kd),
      state, conv_state)


def _sort16_pairs():
    def merge(lo, hi, r):
        step = r * 2
        if step < hi - lo:
            yield from merge(lo, hi, step)
            yield from merge(lo + r, hi, step)
            for i in range(lo + r, hi - r, step):
                yield (i, i + r)
        else:
            yield (lo, lo + r)

    def sort(lo, hi):
        if hi - lo >= 1:
            mid = lo + (hi - lo) // 2
            yield from sort(lo, mid)
            yield from sort(mid + 1, hi)
            yield from merge(lo, hi, 1)

    return tuple(sort(0, PEER_TOPK - 1))


_SORT16 = _sort16_pairs()


def _exchange(vals, i, j):
    a, b = vals[i], vals[j]
    if b is None:
        return
    if a is None:
        vals[i], vals[j] = b, None
        return
    vals[i], vals[j] = jnp.maximum(a, b), jnp.minimum(a, b)


def _sort16_desc(vals):
    vals = list(vals)
    for i, j in _SORT16:
        _exchange(vals, i, j)
    return vals


def _merge_top16(a, b):
    vals = []
    for k in range(PEER_TOPK):
        x, y = a[k], b[PEER_TOPK - 1 - k]
        vals.append(y if x is None else (x if y is None else jnp.maximum(x, y)))
    stride = PEER_TOPK // 2
    while stride:
        for i in range(PEER_TOPK):
            if i & stride == 0:
                _exchange(vals, i, i + stride)
        stride //= 2
    return vals


def _top16_of_rows(s):
    vals = _sort16_desc([s[r * SUBLANES:(r + 1) * SUBLANES, :] for r in range(N_KEYS // SUBLANES)])
    shift = SUBLANES // 2
    while shift:
        vals = _merge_top16(vals, [pltpu.roll(v, shift, axis=0) for v in vals])
        shift //= 2
    return vals


_REST_PAIRS = tuple((k1, k2) for k1 in range(2, PEER_TOPK) for k2 in range(1, PEER_TOPK)
                    if (k1 + 1) * (k2 + 1) <= PEER_TOPK)
assert len(_REST_PAIRS) <= PEER_TOPK


def _top16_sums(v1, v2):
    pad = lambda lst: lst + [None] * (PEER_TOPK - len(lst))
    row0 = [v1[0] + v2[k] for k in range(PEER_TOPK)]
    col0 = pad([v1[k] + v2[0] for k in range(1, PEER_TOPK)])
    row1 = pad([v1[1] + v2[k] for k in range(1, PEER_TOPK) if 2 * (k + 1) <= PEER_TOPK])
    rest = _sort16_desc(pad([v1[a] + v2[b] for a, b in _REST_PAIRS]))
    return _merge_top16(_merge_top16(row0, col0), _merge_top16(row1, rest))


def _count_leading(vals, pred):
    g8 = pred(vals[7])
    g4 = pred(jnp.where(g8, vals[11], vals[3]))
    g2 = pred(jnp.where(g8, jnp.where(g4, vals[13], vals[9]), jnp.where(g4, vals[5], vals[1])))
    quads = [jnp.where(g2, vals[4 * m + 2], vals[4 * m]) for m in range(4)]
    g1 = pred(jnp.where(g8, jnp.where(g4, quads[3], quads[2]), jnp.where(g4, quads[1], quads[0])))
    g16 = pred(vals[15])
    cnt = (jnp.where(g8, jnp.where(g4, 12.0, 8.0), jnp.where(g4, 4.0, 0.0))
           + jnp.where(g2, jnp.where(g1, 3.0, 2.0), jnp.where(g1, 1.0, 0.0)))
    return jnp.where(g16, 16.0, cnt)


def _pair_words(x):
    bits = pltpu.bitcast(x.astype(BF16).astype(F32), jnp.uint32)
    return bits | (bits >> 16)


def _route_lane_block(s1_ref, s2_ref, a_ref, c_ref, b_ref, r_ref, h, lanes):
    s1 = s1_ref[:, lanes]
    s2 = s2_ref[:, lanes]
    v1 = _top16_of_rows(s1)
    v2 = _top16_of_rows(s2)
    top = _top16_sums(v1, v2)
    tau = top[PEER_TOPK - 1]
    z = jnp.ones_like(tau)
    for k in range(1, PEER_TOPK):
        z = z + jnp.exp(top[k] - top[0])
    half_inv_z = 0.5 / z
    for r in range(N_KEYS // SUBLANES):
        rs = slice(r * SUBLANES, (r + 1) * SUBLANES)
        x1, x2 = s1[rs, :], s2[rs, :]
        cnt = _count_leading(v2, lambda v: x1 + v >= tau)
        rank = _count_leading(v2, lambda v: v > x2)
        a_ref[h, rs, lanes] = _pair_words(jnp.exp(x1 - v1[0]) * half_inv_z)
        c_ref[h, rs, lanes] = _pair_words(cnt)
        b_ref[rs, lanes] = jnp.exp(x2 - v2[0])
        r_ref[rs, lanes] = rank


def _bf16_bits(x):
    shape = x.shape
    return pltpu.bitcast(x.reshape(-1, shape[-1]), jnp.int16).reshape(shape)


_GELU_C0 = float(np.sqrt(2.0 / np.pi))
_GELU_C1 = float(np.sqrt(2.0 / np.pi) * 0.044715)


def _twice_gelu_tanh(x):
    return x * (1.0 + jnp.tanh(x * (_GELU_C0 + _GELU_C1 * (x * x))))


def _peer_kernel(rows_per_step, n_prompt_tiles, xp_ref, xs_ref, gffn_ref, wq_ref, k1_ref, k2_ref, u_ref, vt_ref,
                 gfin_ref, yp_ref, ys_ref,
                 ht_ref, acc_ref, s1_ref, s2_ref, a_ref, c_ref, b_ref, r_ref, bb_ref, rb_ref,
                 act_ref, wact_ref):
    t = pl.program_id(0)
    e = pl.program_id(1)
    tokens = xp_ref.shape[0]
    n_lane_blocks = tokens // LANES
    groups = N_KEYS // BF16_ROWS

    def tile_input():
        return jnp.where(t < n_prompt_tiles, xp_ref[...], xs_ref[...])

    @pl.when(e == 0)
    def _route():
        h2 = _rms_rows(tile_input()) * gffn_ref[...]
        ht_ref[...] = h2.T.astype(BF16)
        acc_ref[...] = jnp.zeros_like(acc_ref)

        def head_body(h, carry):
            q0 = pl.multiple_of(h * 2 * D_KEY_HALF, 2 * D_KEY_HALF)
            qt = jnp.dot(wq_ref[pl.ds(q0, 2 * D_KEY_HALF), :], ht_ref[...], preferred_element_type=F32)
            s1_ref[...] = jnp.dot(k1_ref[h], qt[:D_KEY_HALF].astype(BF16), preferred_element_type=F32)
            s2_ref[...] = jnp.dot(k2_ref[h], qt[D_KEY_HALF:].astype(BF16), preferred_element_type=F32)

            def lane_body(lb, c2):
                lanes = pl.ds(pl.multiple_of(lb * LANES, LANES), LANES)
                _route_lane_block(s1_ref, s2_ref, a_ref, c_ref, b_ref, r_ref, h, lanes)
                return c2

            lax.fori_loop(0, n_lane_blocks, lane_body, 0)
            bb_ref[h] = b_ref[...].astype(BF16).reshape(groups, BF16_ROWS, tokens)
            rb_ref[h] = r_ref[...].astype(BF16).reshape(groups, BF16_ROWS, tokens)
            return carry

        lax.fori_loop(0, PEER_HEADS, head_body, 0)

    n_split = 2 if tokens % (2 * 2 * LANES) == 0 else 1
    width = tokens // n_split
    rows_half = UNIT_ROWS
    units = [(eh, sp) for eh in range(rows_per_step // UNIT_ROWS) for sp in range(n_split)]
    row0 = pl.multiple_of(e * rows_per_step, SUBLANES)
    zero = jnp.zeros((), BF16)

    def row_as_bf16(words_ref, h, ii, lanes):
        tile = words_ref[h, pl.ds(row0 + ii // SUBLANES * SUBLANES, SUBLANES), lanes]
        sub = ii % SUBLANES
        return pltpu.bitcast(jnp.broadcast_to(tile[sub:sub + 1, :], (SUBLANES, LANES)), BF16)

    def gate_rows(ii, lanes):
        w = None
        for h in range(PEER_HEADS):
            c16 = row_as_bf16(c_ref, h, ii, lanes)
            a16 = row_as_bf16(a_ref, h, ii, lanes)
            keep = _bf16_bits(rb_ref[h, :, :, lanes]) < _bf16_bits(c16)[None]
            term = jnp.where(keep, bb_ref[h, :, :, lanes], zero) * a16[None]
            w = term if w is None else w + term
        rs = slice(ii * N_KEYS, (ii + 1) * N_KEYS)
        act = _twice_gelu_tanh(act_ref[rs, lanes]).astype(BF16).reshape(groups, BF16_ROWS, LANES)
        wact_ref[rs, lanes] = (w * act).reshape(N_KEYS, LANES)

    for eh, sp in units:
        cols = slice(sp * width, (sp + 1) * width)
        rs = slice(eh * rows_half * N_KEYS, (eh + 1) * rows_half * N_KEYS)
        act_ref[rs, cols] = jnp.dot(u_ref[rs, :], ht_ref[:, cols], preferred_element_type=F32)
    for eh, sp in units:
        for lb in range(width // LANES):
            lane0 = sp * width + lb * LANES
            for ii in range(eh * rows_half, (eh + 1) * rows_half):
                gate_rows(ii, slice(lane0, lane0 + LANES))
    for sp in range(n_split):
        cols = slice(sp * width, (sp + 1) * width)
        acc_ref[:, cols] += jnp.dot(vt_ref[...], wact_ref[:, cols], preferred_element_type=F32)

    @pl.when(e == pl.num_programs(1) - 1)
    def _():
        out = tile_input() + acc_ref[...].T
        y = _rms_rows(out) * gfin_ref[...]

        @pl.when(t < n_prompt_tiles)
        def _():
            yp_ref[...] = y

        @pl.when(t >= n_prompt_tiles)
        def _():
            ys_ref[...] = y


def _peer(x_prompt, x_sample, gffn, wq_t, keys1, keys2, u_tab, v_tab_t, gfin):
    n_prompt_rows, d = x_prompt.shape
    n_sample_rows = x_sample.shape[0]
    total = n_prompt_rows + n_sample_rows
    n_experts = u_tab.shape[0]
    tokens = next(t for t in (512, 256, 128) if n_prompt_rows % t == 0 and n_sample_rows % t == 0)
    n_prompt_tiles = n_prompt_rows // tokens
    prompt_block = lambda t, e: (jnp.minimum(t, n_prompt_tiles - 1), 0)
    sample_block = lambda t, e: (jnp.maximum(t - n_prompt_tiles, 0), 0)
    rows_per_step = ROWS_PER_STEP
    experts_per_step = rows_per_step * N_KEYS
    groups = N_KEYS // BF16_ROWS
    head_shape = (PEER_HEADS, N_KEYS, tokens)
    return pl.pallas_call(
        functools.partial(_peer_kernel, rows_per_step, n_prompt_tiles),
        grid=(total // tokens, n_experts // experts_per_step),
        in_specs=[
            pl.BlockSpec((tokens, d), prompt_block),
            pl.BlockSpec((tokens, d), sample_block),
            _full((1, d)),
            _full(wq_t.shape), _full(keys1.shape), _full(keys2.shape),
            pl.BlockSpec((experts_per_step, d), lambda t, e: (e, 0)),
            pl.BlockSpec((d, experts_per_step), lambda t, e: (0, e)),
            _full((1, d)),
        ],
        out_specs=[pl.BlockSpec((tokens, d), prompt_block), pl.BlockSpec((tokens, d), sample_block)],
        out_shape=[
            jax.ShapeDtypeStruct((n_prompt_rows, d), F32),
            jax.ShapeDtypeStruct((n_sample_rows, d), F32),
        ],
        scratch_shapes=[
            pltpu.VMEM((d, tokens), BF16),
            pltpu.VMEM((d, tokens), F32),
            pltpu.VMEM((N_KEYS, tokens), F32), pltpu.VMEM((N_KEYS, tokens), F32),
            pltpu.VMEM(head_shape, jnp.uint32), pltpu.VMEM(head_shape, jnp.uint32),
            pltpu.VMEM((N_KEYS, tokens), F32), pltpu.VMEM((N_KEYS, tokens), F32),
            pltpu.VMEM((PEER_HEADS, groups, BF16_ROWS, tokens), BF16),
            pltpu.VMEM((PEER_HEADS, groups, BF16_ROWS, tokens), BF16),
            pltpu.VMEM((experts_per_step, tokens), F32),
            pltpu.VMEM((experts_per_step, tokens), BF16),
        ],
        compiler_params=pltpu.CompilerParams(
            dimension_semantics=("arbitrary", "arbitrary"), vmem_limit_bytes=VMEM_LIMIT),
        name="peer",
    )(x_prompt, x_sample, gffn, wq_t, keys1, keys2, u_tab, v_tab_t, gfin)


def kernel(x_prompt, x_sample, state_ret, state_conv, norm_mix_g, w_in, conv_w, w_out, norm_ffn_g,
           peer_wq, peer_keys1, peer_keys2, peer_u, peer_v, norm_final_g):
    depth = w_in.shape[0]
    assert depth == 1
    batch, seq, d = x_prompt.shape
    dec_batch, dec_seq, _ = x_sample.shape

    gmix = norm_mix_g[0][None, :]
    gffn = norm_ffn_g[0][None, :]
    gfin = norm_final_g[None, :]
    w_in_b = w_in[0].astype(BF16)
    w_out_b = w_out[0].astype(BF16)
    wq_t = peer_wq[0].T.astype(BF16)
    keys1 = peer_keys1[0].astype(BF16)
    keys2 = peer_keys2[0].astype(BF16)
    u_tab = peer_u[0].astype(BF16)
    v_tab_t = peer_v[0].T.astype(BF16)

    xp, ret_p, conv_p = _mixer_prompt(x_prompt, gmix, w_in_b, conv_w[0], w_out_b)
    xs, ret_s, conv_s = _mixer_sample(x_sample, state_ret[0], state_conv[0], PAST_LEN,
                                      gmix, w_in_b, conv_w[0], w_out_b)
    y_prompt, y_sample = _peer(xp, xs, gffn, wq_t, keys1, keys2, u_tab, v_tab_t, gfin)
    return (y_prompt.reshape(batch, seq, d), y_sample.reshape(dec_batch, dec_seq, d),
            ret_p[None], conv_p[None], ret_s[None], conv_s[None])
```
